```python
import math
import jax, jax.numpy as jnp
from jax import lax
import numpy as np

D_MODEL = 4096
BATCH = 4
SEQ = 4096
DEPTH = 2

HEAD_DIM = 128
BLOCK_Q = 128
SSM_GROUP = 16
SSM_STATE = 64
W_SSM = D_MODEL // 2
SSM_GROUPS = W_SSM // SSM_GROUP
SB_HEADS = (D_MODEL - W_SSM) // HEAD_DIM
W_SB = SB_HEADS * HEAD_DIM
EVEN_IN = W_SSM + 3 * W_SB
EVEN_OUT = W_SSM + W_SB
FOX_HEADS = D_MODEL // HEAD_DIM
FOX_WIDTH = FOX_HEADS * HEAD_DIM
FOX_IN = 3 * FOX_WIDTH + FOX_HEADS
D_FF = 4 * D_MODEL
N_EVEN = (DEPTH + 1) // 2
N_ODD = DEPTH // 2
DEEPNORM_ALPHA = (2.0 * DEPTH) ** 0.25
DEEPNORM_BETA = (8.0 * DEPTH) ** -0.25
LN_EPS = 1e-5
DT_MIN = 1e-3
DT_MAX = 1e-1
FORGET_BIAS_INIT = 2.0

kernel_name = "hybrid_s5_stickbreak_fox_deepnorm"


def layer_norm(x, g, b):
    xf = x.astype(jnp.float32)
    mu = jnp.mean(xf, axis=-1, keepdims=True)
    var = jnp.mean(jnp.square(xf - mu), axis=-1, keepdims=True)
    y = (xf - mu) * lax.rsqrt(var + LN_EPS) * g.astype(jnp.float32) + b.astype(jnp.float32)
    return y.astype(x.dtype)


def _split_heads(t, n_heads):
    b, s, _ = t.shape
    return t.reshape(b, s, n_heads, HEAD_DIM).transpose(0, 2, 1, 3)


def _to_blocks(t):
    b, h, s = t.shape[:3]
    nb = s // BLOCK_Q
    t = t.reshape((b, h, nb, BLOCK_Q) + t.shape[3:])
    return jnp.moveaxis(t, 2, 0)


def _merge_blocks(o):
    nb, b, h, blk, dh = o.shape
    return o.transpose(1, 0, 3, 2, 4).reshape(b, nb * blk, h * dh)


def _complex_affine_combine(e1, e2):
    a1r, a1i, b1r, b1i = e1
    a2r, a2i, b2r, b2i = e2
    ar = a2r * a1r - a2i * a1i
    ai = a2r * a1i + a2i * a1r
    br = a2r * b1r - a2i * b1i + b2r
    bi = a2r * b1i + a2i * b1r + b2i
    return ar, ai, br, bi


def s5_mixer(u, a_re, a_im, log_dt, b_re, b_im, c_re, c_im, d, w_glu):
    bsz, s, _ = u.shape
    uf = u.astype(jnp.float32).reshape(bsz, s, SSM_GROUPS, SSM_GROUP)
    lr = a_re.astype(jnp.float32)
    li = a_im.astype(jnp.float32)
    dt = jnp.exp(log_dt.astype(jnp.float32))[:, None]
    mag = jnp.exp(lr * dt)
    abar_r = mag * jnp.cos(li * dt)
    abar_i = mag * jnp.sin(li * dt)
    den = lr * lr + li * li
    nr = abar_r - 1.0
    ni = abar_i
    zr = (nr * lr + ni * li) / den
    zi = (ni * lr - nr * li) / den
    br = b_re.astype(jnp.float32)
    bi = b_im.astype(jnp.float32)
    bbar_r = zr[..., None] * br - zi[..., None] * bi
    bbar_i = zr[..., None] * bi + zi[..., None] * br
    bu_r = jnp.einsum('bsgp,gnp->bsgn', uf, bbar_r)
    bu_i = jnp.einsum('bsgp,gnp->bsgn', uf, bbar_i)
    ar = jnp.broadcast_to(abar_r, bu_r.shape)
    ai = jnp.broadcast_to(abar_i, bu_i.shape)
    _, _, xr, xi = lax.associative_scan(_complex_affine_combine, (ar, ai, bu_r, bu_i), axis=1)
    y = (jnp.einsum('bsgn,gpn->bsgp', xr, c_re.astype(jnp.float32))
         - jnp.einsum('bsgn,gpn->bsgp', xi, c_im.astype(jnp.float32))
         + d.astype(jnp.float32).reshape(SSM_GROUPS, SSM_GROUP) * uf)
    y = y.reshape(bsz, s, W_SSM).astype(u.dtype)
    g = jax.nn.gelu(y)
    return g * jax.nn.sigmoid(g @ w_glu)


def stick_breaking_attention(q, k, v):
    s = q.shape[2]
    nb = s // BLOCK_Q
    scale = HEAD_DIM ** -0.5
    key_pos = jnp.arange(s)

    def one_block(args):
        q_blk, blk = args
        z = jnp.einsum('bhqd,bhkd->bhqk', q_blk, k).astype(jnp.float32) * scale
        q_pos = blk * BLOCK_Q + jnp.arange(BLOCK_Q)
        earlier = key_pos[None, :] < q_pos[:, None]
        log_keep = jnp.where(earlier, jax.nn.log_sigmoid(-z), 0.0)
        log_between = lax.cumsum(log_keep, axis=3, reverse=True) - log_keep
        w = jnp.where(earlier, jnp.exp(jax.nn.log_sigmoid(z) + log_between), 0.0)
        return jnp.einsum('bhqk,bhkd->bhqd', w.astype(v.dtype), v)

    out = lax.map(one_block, (_to_blocks(q), jnp.arange(nb)))
    return _merge_blocks(out)


def forgetting_attention(q, k, v, cum):
    s = q.shape[2]
    nb = s // BLOCK_Q
    scale = HEAD_DIM ** -0.5
    key_pos = jnp.arange(s)

    def one_block(args):
        q_blk, c_blk, blk = args
        logits = (jnp.einsum('bhqd,bhkd->bhqk', q_blk, k).astype(jnp.float32) * scale
                  + c_blk[..., :, None] - cum[:, :, None, :])
        q_pos = blk * BLOCK_Q + jnp.arange(BLOCK_Q)
        causal = key_pos[None, :] <= q_pos[:, None]
        p = jax.nn.softmax(jnp.where(causal, logits, -jnp.inf), axis=-1)
        return jnp.einsum('bhqk,bhkd->bhqd', p.astype(v.dtype), v)

    out = lax.map(one_block, (_to_blocks(q), _to_blocks(cum), jnp.arange(nb)))
    return _merge_blocks(out)


def even_mixer(x, w_in, a_re, a_im, log_dt, b_re, b_im, c_re, c_im, d, w_glu, w_out):
    proj = x @ w_in
    u = proj[..., :W_SSM]
    q, k, v = jnp.split(proj[..., W_SSM:], 3, axis=-1)
    y_ssm = s5_mixer(u, a_re, a_im, log_dt, b_re, b_im, c_re, c_im, d, w_glu)
    y_sb = stick_breaking_attention(_split_heads(q, SB_HEADS), _split_heads(k, SB_HEADS),
                                    _split_heads(v, SB_HEADS))
    return jnp.concatenate([y_ssm, y_sb.astype(y_ssm.dtype)], axis=-1) @ w_out


def odd_mixer(x, w_in, b_f, w_out):
    proj = x @ w_in
    q, k, v = jnp.split(proj[..., :3 * FOX_WIDTH], 3, axis=-1)
    f_logit = proj[..., 3 * FOX_WIDTH:].astype(jnp.float32) + b_f.astype(jnp.float32)
    cum = jnp.cumsum(jax.nn.log_sigmoid(f_logit), axis=1).transpose(0, 2, 1)
    y = forgetting_attention(_split_heads(q, FOX_HEADS), _split_heads(k, FOX_HEADS),
                             _split_heads(v, FOX_HEADS), cum)
    return y.astype(x.dtype) @ w_out


def setup_inputs(seed: int = 0) -> dict:
    key = jax.random.key(seed)
    ks = jax.random.split(key, 21)
    nrm = jax.random.normal
    n_idx = jnp.arange(SSM_STATE, dtype=jnp.float32)
    return {
        'x': nrm(ks[0], (BATCH, SEQ, D_MODEL), jnp.float32),
        'even_w_in': nrm(ks[1], (N_EVEN, D_MODEL, EVEN_IN), jnp.float32) * D_MODEL ** -0.5,
        'ssm_a_re': -0.5 + 0.01 * nrm(ks[2], (N_EVEN, SSM_GROUPS, SSM_STATE), jnp.float32),
        'ssm_a_im': math.pi * n_idx + 0.01 * nrm(ks[3], (N_EVEN, SSM_GROUPS, SSM_STATE), jnp.float32),
        'ssm_log_dt': jax.random.uniform(ks[4], (N_EVEN, SSM_GROUPS), jnp.float32,
                                         minval=math.log(DT_MIN), maxval=math.log(DT_MAX)),
        'ssm_b_re': nrm(ks[5], (N_EVEN, SSM_GROUPS, SSM_STATE, SSM_GROUP), jnp.float32) * (2 * SSM_GROUP) ** -0.5,
        'ssm_b_im': nrm(ks[6], (N_EVEN, SSM_GROUPS, SSM_STATE, SSM_GROUP), jnp.float32) * (2 * SSM_GROUP) ** -0.5,
        'ssm_c_re': nrm(ks[7], (N_EVEN, SSM_GROUPS, SSM_GROUP, SSM_STATE), jnp.float32) * (2 * SSM_STATE) ** -0.5,
        'ssm_c_im': nrm(ks[8], (N_EVEN, SSM_GROUPS, SSM_GROUP, SSM_STATE), jnp.float32) * (2 * SSM_STATE) ** -0.5,
        'ssm_d': nrm(ks[9], (N_EVEN, W_SSM), jnp.float32),
        'ssm_w_glu': nrm(ks[10], (N_EVEN, W_SSM, W_SSM), jnp.float32) * W_SSM ** -0.5,
        'even_w_out': nrm(ks[11], (N_EVEN, EVEN_OUT, D_MODEL), jnp.float32) * EVEN_OUT ** -0.5 * DEEPNORM_BETA,
        'fox_w_in': nrm(ks[12], (N_ODD, D_MODEL, FOX_IN), jnp.float32) * D_MODEL ** -0.5,
        'fox_b_f': FORGET_BIAS_INIT + 0.1 * nrm(ks[13], (N_ODD, FOX_HEADS), jnp.float32),
        'fox_w_out': nrm(ks[14], (N_ODD, FOX_WIDTH, D_MODEL), jnp.float32) * FOX_WIDTH ** -0.5 * DEEPNORM_BETA,
        'ln_mix_g': 1.0 + 0.01 * nrm(ks[15], (DEPTH, D_MODEL), jnp.float32),
        'ln_mix_b': 0.01 * nrm(ks[16], (DEPTH, D_MODEL), jnp.float32),
        'mlp_w1': nrm(ks[17], (DEPTH, D_MODEL, D_FF), jnp.float32) * D_MODEL ** -0.5,
        'mlp_w2': nrm(ks[18], (DEPTH, D_FF, D_MODEL), jnp.float32) * D_FF ** -0.5 * DEEPNORM_BETA,
        'ln_ffn_g': 1.0 + 0.01 * nrm(ks[19], (DEPTH, D_MODEL), jnp.float32),
        'ln_ffn_b': 0.01 * nrm(ks[20], (DEPTH, D_MODEL), jnp.float32),
    }


def reference(x, even_w_in, ssm_a_re, ssm_a_im, ssm_log_dt, ssm_b_re, ssm_b_im, ssm_c_re,
              ssm_c_im, ssm_d, ssm_w_glu, even_w_out, fox_w_in, fox_b_f, fox_w_out,
              ln_mix_g, ln_mix_b, mlp_w1, mlp_w2, ln_ffn_g, ln_ffn_b):
    h = x
    for layer in range(DEPTH):
        i = layer // 2
        if layer % 2 == 0:
            mix = even_mixer(h, even_w_in[i], ssm_a_re[i], ssm_a_im[i], ssm_log_dt[i],
                             ssm_b_re[i], ssm_b_im[i], ssm_c_re[i], ssm_c_im[i], ssm_d[i],
                             ssm_w_glu[i], even_w_out[i])
        else:
            mix = odd_mixer(h, fox_w_in[i], fox_b_f[i], fox_w_out[i])
        h = layer_norm(DEEPNORM_ALPHA * h + mix.astype(h.dtype), ln_mix_g[layer], ln_mix_b[layer])
        ff = jnp.square(jax.nn.relu(h @ mlp_w1[layer])) @ mlp_w2[layer]
        h = layer_norm(DEEPNORM_ALPHA * h + ff, ln_ffn_g[layer], ln_ffn_b[layer])
    return h
```

```python
import functools
import math

import jax
import jax.numpy as jnp
from jax import lax
from jax.experimental import pallas as pl
from jax.experimental.pallas import tpu as pltpu

F32 = jnp.float32
BF16 = jnp.bfloat16

HEAD_DIM = 128
LN_EPS = 1e-5
LANES = 128
SUBLANES = 8
VMEM_BYTES = 64 * 1024 * 1024
VMEM_CAP = 56 * 1024 * 1024


def _vmem_limit(estimate_bytes):
    return int(min(VMEM_CAP, max(32 * 1024 * 1024, estimate_bytes * 5 // 4)))


def _log_sigmoid(x):
    return jnp.minimum(x, 0.0) - jnp.log(1.0 + jnp.exp(-jnp.abs(x)))


def _gelu_tanh(x):
    c = math.sqrt(2.0 / math.pi)
    return 0.5 * x * (1.0 + jnp.tanh(c * (x + 0.044715 * (x * x * x))))


def _mm_kernel(*refs, nk, epilogue, alpha, has_extra):
    if has_extra:
        a_ref, b_ref, e_ref, o_ref = refs[:4]
        rest = refs[4:]
    else:
        a_ref, b_ref, o_ref = refs[:3]
        e_ref = None
        rest = refs[3:]

    def finish(acc):
        if epilogue == "cast":
            o_ref[...] = acc.astype(o_ref.dtype)
        elif epilogue == "relu2":
            r = jnp.maximum(acc, 0.0)
            o_ref[...] = (r * r).astype(o_ref.dtype)
        elif epilogue == "glu":
            g = e_ref[...].astype(F32)
            o_ref[...] = (g * (1.0 / (1.0 + jnp.exp(-acc)))).astype(o_ref.dtype)
        elif epilogue == "resid":
            o_ref[...] = (alpha * e_ref[...].astype(F32) + acc).astype(o_ref.dtype)
        else:
            raise ValueError(epilogue)

    if nk == 1:
        finish(jnp.dot(a_ref[...], b_ref[...], preferred_element_type=F32))
        return

    acc_ref = rest[0]
    k = pl.program_id(2)

    @pl.when(k == 0)
    def _():
        acc_ref[...] = jnp.zeros_like(acc_ref)

    acc_ref[...] += jnp.dot(a_ref[...], b_ref[...], preferred_element_type=F32)

    @pl.when(k == nk - 1)
    def _():
        finish(acc_ref[...])


def _matmul(a, b, *, tm, tn, tk, epilogue="cast", extra=None, alpha=1.0, out_dtype=BF16, name):
    m, kdim = a.shape
    k2, n = b.shape
    assert kdim == k2
    tm, tn, tk = min(tm, m), min(tn, n), min(tk, kdim)
    assert m % tm == 0 and n % tn == 0 and kdim % tk == 0, (a.shape, b.shape, tm, tn, tk)
    nk = kdim // tk
    has_extra = extra is not None
    in_specs = [pl.BlockSpec((tm, tk), lambda i, j, k: (i, k)),
                pl.BlockSpec((tk, tn), lambda i, j, k: (k, j))]
    args = [a, b]
    est = 2 * (tm * tk * a.dtype.itemsize + tk * tn * b.dtype.itemsize)
    est += 2 * tm * tn * jnp.dtype(out_dtype).itemsize + 2 * tm * tn * 4
    if has_extra:
        in_specs.append(pl.BlockSpec((tm, tn), lambda i, j, k: (i, j)))
        args.append(extra)
        est += 2 * tm * tn * extra.dtype.itemsize
    scratch = [pltpu.VMEM((tm, tn), F32)] if nk > 1 else []
    return pl.pallas_call(
        functools.partial(_mm_kernel, nk=nk, epilogue=epilogue, alpha=alpha, has_extra=has_extra),
        grid=(m // tm, n // tn, nk),
        in_specs=in_specs,
        out_specs=pl.BlockSpec((tm, tn), lambda i, j, k: (i, j)),
        out_shape=jax.ShapeDtypeStruct((m, n), out_dtype),
        scratch_shapes=scratch,
        compiler_params=pltpu.CompilerParams(
            dimension_semantics=("parallel", "parallel", "arbitrary"),
            vmem_limit_bytes=_vmem_limit(est)),
        name=name,
    )(*args)


def _ln_kernel(x_ref, g_ref, b_ref, of_ref, ob_ref):
    x = x_ref[...]
    mu = jnp.mean(x, axis=-1, keepdims=True)
    xc = x - mu
    var = jnp.mean(xc * xc, axis=-1, keepdims=True)
    y = xc * lax.rsqrt(var + LN_EPS) * g_ref[...] + b_ref[...]
    of_ref[...] = y
    ob_ref[...] = y.astype(BF16)


def _layer_norm(x, g, b, *, tr, name):
    m, d = x.shape
    tr = min(tr, m)
    assert m % tr == 0
    row = pl.BlockSpec((tr, d), lambda i: (i, 0))
    vec = pl.BlockSpec((1, d), lambda i: (0, 0))
    return pl.pallas_call(
        _ln_kernel,
        grid=(m // tr,),
        in_specs=[row, vec, vec],
        out_specs=[row, row],
        out_shape=[jax.ShapeDtypeStruct((m, d), F32), jax.ShapeDtypeStruct((m, d), BF16)],
        compiler_params=pltpu.CompilerParams(
            dimension_semantics=("parallel",),
            vmem_limit_bytes=_vmem_limit(2 * tr * d * (4 + 4 + 2) + 4 * tr * d * 4)),
        name=name,
    )(x, g.reshape(1, d).astype(F32), b.reshape(1, d).astype(F32))


def _ssm_kernel(u_ref, bm_ref, cm_ref, ar_ref, ai_ref, d_ref, o_ref, sre, sim, st_ref,
                *, lc, pitch, nq, tiles_per_q, n_vregs):
    c = pl.program_id(1)

    @pl.when(c == 0)
    def _():
        st_ref[...] = jnp.zeros_like(st_ref)

    for q in range(nq):
        uq = u_ref[0, :, q * LANES:(q + 1) * LANES]
        res = jnp.dot(uq, bm_ref[q], preferred_element_type=F32)
        for tl in range(tiles_per_q):
            base = (q * tiles_per_q + tl) * pitch
            sre[base:base + lc, :] = res[:, (2 * tl) * LANES:(2 * tl + 1) * LANES]
            sim[base:base + lc, :] = res[:, (2 * tl + 1) * LANES:(2 * tl + 2) * LANES]

    ar = [ar_ref[j * SUBLANES:(j + 1) * SUBLANES, :] for j in range(n_vregs)]
    ai = [ai_ref[j * SUBLANES:(j + 1) * SUBLANES, :] for j in range(n_vregs)]
    xr0 = tuple(st_ref[0, j * SUBLANES:(j + 1) * SUBLANES, :] for j in range(n_vregs))
    xi0 = tuple(st_ref[1, j * SUBLANES:(j + 1) * SUBLANES, :] for j in range(n_vregs))

    def step(t, carry):
        xr, xi = carry
        nxr, nxi = [], []
        for j in range(n_vregs):
            rows = pl.ds(j * SUBLANES * pitch + t, SUBLANES, stride=pitch)
            nr = ar[j] * xr[j] - ai[j] * xi[j] + sre[rows, :]
            ni = ar[j] * xi[j] + ai[j] * xr[j] + sim[rows, :]
            sre[rows, :] = nr
            sim[rows, :] = ni
            nxr.append(nr)
            nxi.append(ni)
        return tuple(nxr), tuple(nxi)

    xr, xi = lax.fori_loop(0, lc, step, (xr0, xi0))
    for j in range(n_vregs):
        st_ref[0, j * SUBLANES:(j + 1) * SUBLANES, :] = xr[j]
        st_ref[1, j * SUBLANES:(j + 1) * SUBLANES, :] = xi[j]

    for q in range(nq):
        pieces = []
        for tl in range(tiles_per_q):
            base = (q * tiles_per_q + tl) * pitch
            pieces.append(sre[base:base + lc, :].astype(BF16))
            pieces.append(sim[base:base + lc, :].astype(BF16))
        xcat = jnp.concatenate(pieces, axis=1)
        y = jnp.dot(xcat, cm_ref[q], preferred_element_type=F32)
        uq = u_ref[0, :, q * LANES:(q + 1) * LANES].astype(F32)
        y = y + d_ref[:, q * LANES:(q + 1) * LANES] * uq
        o_ref[0, :, q * LANES:(q + 1) * LANES] = _gelu_tanh(y).astype(o_ref.dtype)


def _ssm_discretise(a_re, a_im, log_dt, b_re, b_im, c_re, c_im):
    g, n = a_re.shape
    p = b_re.shape[-1]
    lr = a_re.astype(F32)
    li = a_im.astype(F32)
    dt = jnp.exp(log_dt.astype(F32))[:, None]
    mag = jnp.exp(lr * dt)
    abar_r = mag * jnp.cos(li * dt)
    abar_i = mag * jnp.sin(li * dt)
    den = lr * lr + li * li
    nr = abar_r - 1.0
    ni = abar_i
    zr = (nr * lr + ni * li) / den
    zi = (ni * lr - nr * li) / den
    br = b_re.astype(F32)
    bi = b_im.astype(F32)
    bbar_r = zr[..., None] * br - zi[..., None] * bi
    bbar_i = zr[..., None] * bi + zi[..., None] * br

    gq = LANES // p
    gt = LANES // n
    nq = g // gq
    tq = gq // gt
    eye = jnp.eye(gq, dtype=F32)
    bb = jnp.stack([bbar_r, bbar_i], 0).reshape(2, nq, gq, n, p)
    bf = jnp.einsum("aqhnp,gh->qgpahn", bb, eye)
    bf = bf.reshape(nq, gq, p, 2, tq, gt, n).transpose(0, 1, 2, 4, 3, 5, 6)
    bmat = bf.reshape(nq, gq * p, tq * 2 * gt * n).astype(BF16)
    cc = jnp.stack([c_re.astype(F32), -c_im.astype(F32)], 0).reshape(2, nq, gq, p, n)
    cf = jnp.einsum("aqhpn,gh->qahngp", cc, eye)
    cf = cf.reshape(nq, 2, tq, gt, n, gq, p).transpose(0, 2, 1, 3, 4, 5, 6)
    cmat = cf.reshape(nq, tq * 2 * gt * n, gq * p).astype(BF16)
    n_tiles = g * n // LANES
    return bmat, cmat, abar_r.reshape(n_tiles, LANES), abar_i.reshape(n_tiles, LANES)


def _ssm_scan(proj, bmat, cmat, ar, ai, d, *, w_ssm, lc):
    bsz, s, _ = proj.shape
    lc = min(lc, s)
    assert s % lc == 0 and lc % SUBLANES == 0
    nq, _, qcols = bmat.shape
    tiles_per_q = qcols // (2 * LANES)
    n_tiles = ar.shape[0]
    assert n_tiles == nq * tiles_per_q and n_tiles % SUBLANES == 0 and w_ssm == nq * LANES
    n_vregs = n_tiles // SUBLANES
    pitch = lc + 4
    est = (2 * 2 * lc * w_ssm * 2 + 2 * (bmat.size + cmat.size) * 2
           + 2 * n_tiles * pitch * LANES * 4 + 4 * lc * qcols * 4)
    kern = functools.partial(_ssm_kernel, lc=lc, pitch=pitch, nq=nq, tiles_per_q=tiles_per_q,
                             n_vregs=n_vregs)
    const2 = lambda b, c: (0, 0)
    const3 = lambda b, c: (0, 0, 0)
    return pl.pallas_call(
        kern,
        grid=(bsz, s // lc),
        in_specs=[pl.BlockSpec((1, lc, w_ssm), lambda b, c: (b, c, 0)),
                  pl.BlockSpec(bmat.shape, const3),
                  pl.BlockSpec(cmat.shape, const3),
                  pl.BlockSpec(ar.shape, const2),
                  pl.BlockSpec(ai.shape, const2),
                  pl.BlockSpec((1, w_ssm), const2)],
        out_specs=pl.BlockSpec((1, lc, w_ssm), lambda b, c: (b, c, 0)),
        out_shape=jax.ShapeDtypeStruct((bsz, s, w_ssm), BF16),
        scratch_shapes=[pltpu.VMEM((n_tiles * pitch, LANES), F32),
                        pltpu.VMEM((n_tiles * pitch, LANES), F32),
                        pltpu.VMEM((2, n_tiles, LANES), F32)],
        compiler_params=pltpu.CompilerParams(
            dimension_semantics=("parallel", "arbitrary"),
            vmem_limit_bytes=_vmem_limit(est)),
        name="s5_scan",
    )(proj, bmat, cmat, ar, ai, d.reshape(1, w_ssm).astype(F32))


def _sb_kernel(q_ref, k_ref, v_ref, o_ref, *, tq, scale):
    qi = pl.program_id(2)
    q = q_ref[0]
    row = lax.broadcasted_iota(jnp.int32, (tq, tq), 0)
    col = lax.broadcasted_iota(jnp.int32, (tq, tq), 1)
    later = (row > col).astype(BF16)
    earlier = col < row

    def block(kb, carry, acc, masked):
        start = pl.multiple_of(kb * tq, tq)
        k = k_ref[0, pl.ds(start, tq), :]
        v = v_ref[0, pl.ds(start, tq), :]
        z = lax.dot_general(q, k, (((1,), (1,)), ((), ())), preferred_element_type=F32) * scale
        ls = _log_sigmoid(z)
        lk = ls - z
        if masked:
            lk = jnp.where(earlier, lk, 0.0)
        hi = lk.astype(BF16)
        lo = (lk - hi.astype(F32)).astype(BF16)
        between = (jnp.dot(hi, later, preferred_element_type=F32)
                   + jnp.dot(lo, later, preferred_element_type=F32) + carry)
        w = jnp.exp(ls + between)
        if masked:
            w = jnp.where(earlier, w, 0.0)
        acc = acc + jnp.dot(w.astype(BF16), v, preferred_element_type=F32)
        carry = carry + jnp.sum(lk, axis=1, keepdims=True)
        return carry, acc

    carry, acc = block(qi, jnp.zeros((tq, 1), F32), jnp.zeros((tq, HEAD_DIM), F32), True)

    def body(it, state):
        return block(qi - 1 - it, state[0], state[1], False)

    carry, acc = lax.fori_loop(0, qi, body, (carry, acc))
    o_ref[0] = acc.astype(o_ref.dtype)


def _sb_attention(proj, *, heads, col0, tq):
    bsz, s, _ = proj.shape
    tq = min(tq, s)
    assert s % tq == 0 and col0 % HEAD_DIM == 0
    c0 = col0 // HEAD_DIM
    est = 2 * (2 * s * HEAD_DIM * 2) + 4 * tq * HEAD_DIM * 2 + 12 * tq * tq * 4
    return pl.pallas_call(
        functools.partial(_sb_kernel, tq=tq, scale=HEAD_DIM ** -0.5),
        grid=(bsz, heads, s // tq),
        in_specs=[pl.BlockSpec((1, tq, HEAD_DIM), lambda b, h, i: (b, i, c0 + h)),
                  pl.BlockSpec((1, s, HEAD_DIM), lambda b, h, i: (b, 0, c0 + heads + h)),
                  pl.BlockSpec((1, s, HEAD_DIM), lambda b, h, i: (b, 0, c0 + 2 * heads + h))],
        out_specs=pl.BlockSpec((1, tq, HEAD_DIM), lambda b, h, i: (b, i, h)),
        out_shape=jax.ShapeDtypeStruct((bsz, s, heads * HEAD_DIM), BF16),
        compiler_params=pltpu.CompilerParams(
            dimension_semantics=("parallel", "parallel", "arbitrary"),
            vmem_limit_bytes=_vmem_limit(est)),
        name="stickbreak_attn",
    )(proj, proj, proj)


def _cum_kernel(f_ref, b_ref, o_ref, *, chunk):
    s = f_ref.shape[1]
    row = lax.broadcasted_iota(jnp.int32, (chunk, chunk), 0)
    col = lax.broadcasted_iota(jnp.int32, (chunk, chunk), 1)
    tri = (col <= row).astype(F32)
    carry = jnp.zeros((1, f_ref.shape[2]), F32)
    for c in range(s // chunk):
        ls = _log_sigmoid(f_ref[0, c * chunk:(c + 1) * chunk, :] + b_ref[...])
        cs = jnp.dot(tri, ls, preferred_element_type=F32, precision=lax.Precision.HIGHEST) + carry
        o_ref[0, c * chunk:(c + 1) * chunk, :] = cs
        carry = cs[chunk - 1:chunk, :]


def _forget_cumsum(f_logit, b_f, *, chunk=256):
    bsz, s, w = f_logit.shape
    chunk = min(chunk, s)
    assert s % chunk == 0
    blk = pl.BlockSpec((1, s, w), lambda b: (b, 0, 0))
    return pl.pallas_call(
        functools.partial(_cum_kernel, chunk=chunk),
        grid=(bsz,),
        in_specs=[blk, pl.BlockSpec((1, w), lambda b: (0, 0))],
        out_specs=blk,
        out_shape=jax.ShapeDtypeStruct((bsz, s, w), F32),
        compiler_params=pltpu.CompilerParams(dimension_semantics=("parallel",)),
        name="forget_cumsum",
    )(f_logit, b_f)


def _fox_kernel(q_ref, k_ref, v_ref, cc_ref, cr_ref, o_ref, *, tq, scale):
    qi = pl.program_id(2)
    q = q_ref[0]
    ci = cc_ref[0, 0]
    row = lax.broadcasted_iota(jnp.int32, (tq, tq), 0)
    col = lax.broadcasted_iota(jnp.int32, (tq, tq), 1)
    causal = col <= row

    def block(kb, m, l, acc, masked):
        start = pl.multiple_of(kb * tq, tq)
        k = k_ref[0, pl.ds(start, tq), :]
        v = v_ref[0, pl.ds(start, tq), :]
        cj = cr_ref[0, 0, pl.ds(kb, 1), :]
        z = lax.dot_general(q, k, (((1,), (1,)), ((), ())), preferred_element_type=F32) * scale
        z = z + ci - cj
        if masked:
            z = jnp.where(causal, z, -jnp.inf)
        m_new = jnp.maximum(m, jnp.max(z, axis=1, keepdims=True))
        a = jnp.exp(m - m_new)
        p = jnp.exp(z - m_new)
        l = a * l + jnp.sum(p, axis=1, keepdims=True)
        acc = a * acc + jnp.dot(p.astype(BF16), v, preferred_element_type=F32)
        return m_new, l, acc

    m0 = jnp.full((tq, 1), -jnp.inf, F32)
    m, l, acc = block(qi, m0, jnp.zeros((tq, 1), F32), jnp.zeros((tq, HEAD_DIM), F32), True)

    def body(it, state):
        return block(qi - 1 - it, state[0], state[1], state[2], False)

    m, l, acc = lax.fori_loop(0, qi, body, (m, l, acc))
    o_ref[0] = (acc / l).astype(o_ref.dtype)


def _fox_attention(qkv, cum, *, heads, tq):
    bsz, s, _ = qkv.shape
    tq = min(tq, s)
    assert s % tq == 0
    nb = s // tq
    cum_col = cum.reshape(bsz, heads, s, 1)
    cum_row = cum.reshape(bsz, heads, nb, tq)
    est = 2 * (2 * s * HEAD_DIM * 2) + 4 * tq * HEAD_DIM * 2 + 2 * tq * LANES * 4 + 12 * tq * tq * 4
    return pl.pallas_call(
        functools.partial(_fox_kernel, tq=tq, scale=HEAD_DIM ** -0.5),
        grid=(bsz, heads, nb),
        in_specs=[pl.BlockSpec((1, tq, HEAD_DIM), lambda b, h, i: (b, i, h)),
                  pl.BlockSpec((1, s, HEAD_DIM), lambda b, h, i: (b, 0, heads + h)),
                  pl.BlockSpec((1, s, HEAD_DIM), lambda b, h, i: (b, 0, 2 * heads + h)),
                  pl.BlockSpec((1, 1, tq, 1), lambda b, h, i: (b, h, i, 0)),
                  pl.BlockSpec((1, 1, nb, tq), lambda b, h, i: (b, h, 0, 0))],
        out_specs=pl.BlockSpec((1, tq, HEAD_DIM), lambda b, h, i: (b, i, h)),
        out_shape=jax.ShapeDtypeStruct((bsz, s, heads * HEAD_DIM), BF16),
        compiler_params=pltpu.CompilerParams(
            dimension_semantics=("parallel", "parallel", "arbitrary"),
            vmem_limit_bytes=_vmem_limit(est)),
        name="fox_attn",
    )(qkv, qkv, qkv, cum_col, cum_row)


def _mlp_and_norms(h32, hb, mix_fn, ln_mix_g, ln_mix_b, w1, w2, ln_ffn_g, ln_ffn_b, *, alpha, tag):
    z = mix_fn(h32, hb)
    h32, hb = _layer_norm(z, ln_mix_g, ln_mix_b, tr=256, name=f"ln_mix_{tag}")
    hid = _matmul(hb, w1.astype(BF16), tm=1024, tn=512, tk=4096, epilogue="relu2", name=f"mlp_up_{tag}")
    z = _matmul(hid, w2.astype(BF16), tm=1024, tn=1024, tk=2048, epilogue="resid", extra=h32,
                alpha=alpha, out_dtype=F32, name=f"mlp_down_{tag}")
    return _layer_norm(z, ln_ffn_g, ln_ffn_b, tr=256, name=f"ln_ffn_{tag}")


def kernel(x, even_w_in, ssm_a_re, ssm_a_im, ssm_log_dt, ssm_b_re, ssm_b_im, ssm_c_re, ssm_c_im, ssm_d,
           ssm_w_glu, even_w_out, fox_w_in, fox_b_f, fox_w_out, ln_mix_g, ln_mix_b, mlp_w1, mlp_w2,
           ln_ffn_g, ln_ffn_b):
    bsz, s, d_model = x.shape
    depth = ln_mix_g.shape[0]
    alpha = (2.0 * depth) ** 0.25
    m = bsz * s
    w_ssm = ssm_d.shape[-1]

    h32 = x.reshape(m, d_model)
    hb = h32.astype(BF16)
    for layer in range(depth):
        i = layer // 2
        if layer % 2 == 0:
            def mix_fn(h32, hb, i=i):
                sb_heads = (even_w_in.shape[-1] - w_ssm) // (3 * HEAD_DIM)
                proj = _matmul(hb, even_w_in[i].astype(BF16), tm=1024, tn=512, tk=4096, name="even_in")
                proj3 = proj.reshape(bsz, s, -1)
                bmat, cmat, ar, ai = _ssm_discretise(ssm_a_re[i], ssm_a_im[i], ssm_log_dt[i], ssm_b_re[i],
                                                     ssm_b_im[i], ssm_c_re[i], ssm_c_im[i])
                g = _ssm_scan(proj3, bmat, cmat, ar, ai, ssm_d[i], w_ssm=w_ssm, lc=256).reshape(m, w_ssm)
                y_ssm = _matmul(g, ssm_w_glu[i].astype(BF16), tm=1024, tn=512, tk=2048, epilogue="glu",
                                extra=g, name="ssm_glu")
                y_sb = _sb_attention(proj3, heads=sb_heads, col0=w_ssm, tq=256).reshape(m, -1)
                y = jnp.concatenate([y_ssm, y_sb], axis=1)
                return _matmul(y, even_w_out[i].astype(BF16), tm=1024, tn=1024, tk=2048, epilogue="resid",
                               extra=h32, alpha=alpha, out_dtype=F32, name="even_out")
        else:
            def mix_fn(h32, hb, i=i):
                heads = fox_b_f.shape[-1]
                width = heads * HEAD_DIM
                w_in = fox_w_in[i]
                qkv = _matmul(hb, w_in[:, :3 * width].astype(BF16), tm=1024, tn=512, tk=4096, name="fox_in")
                w_f = jnp.pad(w_in[:, 3 * width:], ((0, 0), (0, LANES - heads))).astype(BF16)
                b_f = jnp.pad(fox_b_f[i].astype(F32), (0, LANES - heads)).reshape(1, LANES)
                f_logit = _matmul(hb, w_f, tm=1024, tn=LANES, tk=4096, out_dtype=F32, name="fox_forget")
                cum = _forget_cumsum(f_logit.reshape(bsz, s, LANES), b_f)
                cum = cum[:, :, :heads].transpose(0, 2, 1)
                y = _fox_attention(qkv.reshape(bsz, s, 3 * width), cum, heads=heads, tq=256)
                return _matmul(y.reshape(m, width), fox_w_out[i].astype(BF16), tm=1024, tn=1024, tk=2048,
                               epilogue="resid", extra=h32, alpha=alpha, out_dtype=F32, name="fox_out")
        h32, hb = _mlp_and_norms(h32, hb, mix_fn, ln_mix_g[layer], ln_mix_b[layer], mlp_w1[layer],
                                 mlp_w2[layer], ln_ffn_g[layer], ln_ffn_b[layer], alpha=alpha, tag=str(layer))
    return h32.reshape(bsz, s, d_model)
```

```python
import functools
import math

import jax
import jax.numpy as jnp
from jax import lax
from jax.experimental import pallas as pl
from jax.experimental.pallas import tpu as pltpu

F32 = jnp.float32
BF16 = jnp.bfloat16

HEAD_DIM = 128
LN_EPS = 1e-5
LOG2E = math.log2(math.e)
LANES = 128
SUBLANES = 8
MXU_DIM = 256
VMEM_CAP = 56 * 1024 * 1024
CUM_PIECES = 3

_NT = (((1,), (1,)), ((), ()))
_TN = (((0,), (0,)), ((), ()))


def _vmem_limit(estimate_bytes):
    return int(min(VMEM_CAP, max(32 * 1024 * 1024, estimate_bytes * 5 // 4)))


def _log_sigmoid(x):
    return jnp.minimum(x, 0.0) - jnp.log(1.0 + jnp.exp(-jnp.abs(x)))


def _gelu_tanh(x):
    c = math.sqrt(2.0 / math.pi)
    return 0.5 * x * (1.0 + jnp.tanh(c * (x + 0.044715 * (x * x * x))))


def _mm_kernel(*refs, nk, epilogue, alpha, has_extra):
    if has_extra:
        a_ref, b_ref, e_ref, o_ref = refs[:4]
        rest = refs[4:]
    else:
        a_ref, b_ref, o_ref = refs[:3]
        e_ref = None
        rest = refs[3:]

    def finish(acc):
        if epilogue == "cast":
            o_ref[...] = acc.astype(o_ref.dtype)
        elif epilogue == "colscale":
            o_ref[...] = (acc * e_ref[...]).astype(o_ref.dtype)
        elif epilogue == "relu2":
            r = jnp.maximum(acc, 0.0)
            o_ref[...] = (r * r).astype(o_ref.dtype)
        elif epilogue == "glu":
            g = e_ref[...].astype(F32)
            o_ref[...] = (g * (1.0 / (1.0 + jnp.exp(-acc)))).astype(o_ref.dtype)
        elif epilogue == "resid":
            o_ref[...] = (alpha * e_ref[...].astype(F32) + acc).astype(o_ref.dtype)
        else:
            raise ValueError(epilogue)

    if nk == 1:
        finish(jnp.dot(a_ref[...], b_ref[...], preferred_element_type=F32))
        return

    acc_ref = rest[0]
    k = pl.program_id(2)

    @pl.when(k == 0)
    def _():
        acc_ref[...] = jnp.zeros_like(acc_ref)

    acc_ref[...] += jnp.dot(a_ref[...], b_ref[...], preferred_element_type=F32)

    @pl.when(k == nk - 1)
    def _():
        finish(acc_ref[...])


def _matmul(a, b, *, tm, tn, tk, epilogue="cast", extra=None, alpha=1.0, out_dtype=BF16, name):
    m, kdim = a.shape
    k2, n = b.shape
    assert kdim == k2
    tm, tn, tk = min(tm, m), min(tn, n), min(tk, kdim)
    assert m % tm == 0 and n % tn == 0 and kdim % tk == 0, (a.shape, b.shape, tm, tn, tk)
    nk = kdim // tk
    has_extra = extra is not None
    in_specs = [pl.BlockSpec((tm, tk), lambda i, j, k: (i, k)),
                pl.BlockSpec((tk, tn), lambda i, j, k: (k, j))]
    args = [a, b]
    est = 2 * (tm * tk * a.dtype.itemsize + tk * tn * b.dtype.itemsize)
    est += 2 * tm * tn * jnp.dtype(out_dtype).itemsize + 2 * tm * tn * 4
    if has_extra:
        if epilogue == "colscale":
            in_specs.append(pl.BlockSpec((1, tn), lambda i, j, k: (0, j)))
        else:
            in_specs.append(pl.BlockSpec((tm, tn), lambda i, j, k: (i, j)))
            est += 2 * tm * tn * extra.dtype.itemsize
        args.append(extra)
    scratch = [pltpu.VMEM((tm, tn), F32)] if nk > 1 else []
    return pl.pallas_call(
        functools.partial(_mm_kernel, nk=nk, epilogue=epilogue, alpha=alpha, has_extra=has_extra),
        grid=(m // tm, n // tn, nk),
        in_specs=in_specs,
        out_specs=pl.BlockSpec((tm, tn), lambda i, j, k: (i, j)),
        out_shape=jax.ShapeDtypeStruct((m, n), out_dtype),
        scratch_shapes=scratch,
        compiler_params=pltpu.CompilerParams(
            dimension_semantics=("parallel", "parallel", "arbitrary"),
            vmem_limit_bytes=_vmem_limit(est)),
        name=name,
    )(*args)


def _query_colscale(n_cols, q_start, q_width):
    col = jnp.arange(n_cols)
    is_q = (col >= q_start) & (col < q_start + q_width)
    return jnp.where(is_q, HEAD_DIM ** -0.5 * LOG2E, 1.0).astype(F32).reshape(1, n_cols)


def _ln_kernel(x_ref, g_ref, b_ref, of_ref, ob_ref):
    x = x_ref[...]
    mu = jnp.mean(x, axis=-1, keepdims=True)
    xc = x - mu
    var = jnp.mean(xc * xc, axis=-1, keepdims=True)
    y = xc * lax.rsqrt(var + LN_EPS) * g_ref[...] + b_ref[...]
    of_ref[...] = y
    ob_ref[...] = y.astype(BF16)


def _layer_norm(x, g, b, *, tr, name):
    m, d = x.shape
    tr = min(tr, m)
    assert m % tr == 0
    row = pl.BlockSpec((tr, d), lambda i: (i, 0))
    vec = pl.BlockSpec((1, d), lambda i: (0, 0))
    return pl.pallas_call(
        _ln_kernel,
        grid=(m // tr,),
        in_specs=[row, vec, vec],
        out_specs=[row, row],
        out_shape=[jax.ShapeDtypeStruct((m, d), F32), jax.ShapeDtypeStruct((m, d), BF16)],
        compiler_params=pltpu.CompilerParams(
            dimension_semantics=("parallel",),
            vmem_limit_bytes=_vmem_limit(2 * tr * d * (4 + 4 + 2) + 4 * tr * d * 4)),
        name=name,
    )(x, g.reshape(1, d).astype(F32), b.reshape(1, d).astype(F32))


def _ssm_kernel(u_ref, bm_ref, cm_ref, ar_ref, ai_ref, d_ref, o_ref, sre, sim, st_ref,
                *, lc, pitch, nq, tiles_per_q, n_vregs):
    c = pl.program_id(1)

    @pl.when(c == 0)
    def _():
        st_ref[...] = jnp.zeros_like(st_ref)

    for q in range(nq):
        uq = u_ref[0, :, q * LANES:(q + 1) * LANES]
        res = jnp.dot(uq, bm_ref[q], preferred_element_type=F32)
        for tl in range(tiles_per_q):
            base = (q * tiles_per_q + tl) * pitch
            sre[base:base + lc, :] = res[:, (2 * tl) * LANES:(2 * tl + 1) * LANES]
            sim[base:base + lc, :] = res[:, (2 * tl + 1) * LANES:(2 * tl + 2) * LANES]

    ar = [ar_ref[j * SUBLANES:(j + 1) * SUBLANES, :] for j in range(n_vregs)]
    ai = [ai_ref[j * SUBLANES:(j + 1) * SUBLANES, :] for j in range(n_vregs)]
    xr0 = tuple(st_ref[0, j * SUBLANES:(j + 1) * SUBLANES, :] for j in range(n_vregs))
    xi0 = tuple(st_ref[1, j * SUBLANES:(j + 1) * SUBLANES, :] for j in range(n_vregs))

    def step(t, carry):
        xr, xi = carry
        nxr, nxi = [], []
        for j in range(n_vregs):
            rows = pl.ds(j * SUBLANES * pitch + t, SUBLANES, stride=pitch)
            nr = ar[j] * xr[j] - ai[j] * xi[j] + sre[rows, :]
            ni = ar[j] * xi[j] + ai[j] * xr[j] + sim[rows, :]
            sre[rows, :] = nr
            sim[rows, :] = ni
            nxr.append(nr)
            nxi.append(ni)
        return tuple(nxr), tuple(nxi)

    xr, xi = lax.fori_loop(0, lc, step, (xr0, xi0))
    for j in range(n_vregs):
        st_ref[0, j * SUBLANES:(j + 1) * SUBLANES, :] = xr[j]
        st_ref[1, j * SUBLANES:(j + 1) * SUBLANES, :] = xi[j]

    for q in range(nq):
        pieces = []
        for tl in range(tiles_per_q):
            base = (q * tiles_per_q + tl) * pitch
            pieces.append(sre[base:base + lc, :].astype(BF16))
            pieces.append(sim[base:base + lc, :].astype(BF16))
        xcat = jnp.concatenate(pieces, axis=1)
        y = jnp.dot(xcat, cm_ref[q], preferred_element_type=F32)
        uq = u_ref[0, :, q * LANES:(q + 1) * LANES].astype(F32)
        y = y + d_ref[:, q * LANES:(q + 1) * LANES] * uq
        o_ref[0, :, q * LANES:(q + 1) * LANES] = _gelu_tanh(y).astype(o_ref.dtype)


def _ssm_discretise(a_re, a_im, log_dt, b_re, b_im, c_re, c_im):
    g, n = a_re.shape
    p = b_re.shape[-1]
    lr = a_re.astype(F32)
    li = a_im.astype(F32)
    dt = jnp.exp(log_dt.astype(F32))[:, None]
    mag = jnp.exp(lr * dt)
    abar_r = mag * jnp.cos(li * dt)
    abar_i = mag * jnp.sin(li * dt)
    den = lr * lr + li * li
    nr = abar_r - 1.0
    ni = abar_i
    zr = (nr * lr + ni * li) / den
    zi = (ni * lr - nr * li) / den
    br = b_re.astype(F32)
    bi = b_im.astype(F32)
    bbar_r = zr[..., None] * br - zi[..., None] * bi
    bbar_i = zr[..., None] * bi + zi[..., None] * br

    gq = LANES // p
    gt = LANES // n
    nq = g // gq
    tq = gq // gt
    eye = jnp.eye(gq, dtype=F32)
    bb = jnp.stack([bbar_r, bbar_i], 0).reshape(2, nq, gq, n, p)
    bf = jnp.einsum("aqhnp,gh->qgpahn", bb, eye)
    bf = bf.reshape(nq, gq, p, 2, tq, gt, n).transpose(0, 1, 2, 4, 3, 5, 6)
    bmat = bf.reshape(nq, gq * p, tq * 2 * gt * n).astype(BF16)
    cc = jnp.stack([c_re.astype(F32), -c_im.astype(F32)], 0).reshape(2, nq, gq, p, n)
    cf = jnp.einsum("aqhpn,gh->qahngp", cc, eye)
    cf = cf.reshape(nq, 2, tq, gt, n, gq, p).transpose(0, 2, 1, 3, 4, 5, 6)
    cmat = cf.reshape(nq, tq * 2 * gt * n, gq * p).astype(BF16)
    n_tiles = g * n // LANES
    return bmat, cmat, abar_r.reshape(n_tiles, LANES), abar_i.reshape(n_tiles, LANES)


def _ssm_scan(proj, bmat, cmat, ar, ai, d, *, w_ssm, lc):
    bsz, s, _ = proj.shape
    lc = min(lc, s)
    assert s % lc == 0 and lc % SUBLANES == 0
    nq, _, qcols = bmat.shape
    tiles_per_q = qcols // (2 * LANES)
    n_tiles = ar.shape[0]
    assert n_tiles == nq * tiles_per_q and n_tiles % SUBLANES == 0 and w_ssm == nq * LANES
    n_vregs = n_tiles // SUBLANES
    pitch = lc + 4
    est = (2 * 2 * lc * w_ssm * 2 + 2 * (bmat.size + cmat.size) * 2
           + 2 * n_tiles * pitch * LANES * 4 + 4 * lc * qcols * 4)
    kern = functools.partial(_ssm_kernel, lc=lc, pitch=pitch, nq=nq, tiles_per_q=tiles_per_q,
                             n_vregs=n_vregs)
    const2 = lambda b, c: (0, 0)
    const3 = lambda b, c: (0, 0, 0)
    return pl.pallas_call(
        kern,
        grid=(bsz, s // lc),
        in_specs=[pl.BlockSpec((1, lc, w_ssm), lambda b, c: (b, c, 0)),
                  pl.BlockSpec(bmat.shape, const3),
                  pl.BlockSpec(cmat.shape, const3),
                  pl.BlockSpec(ar.shape, const2),
                  pl.BlockSpec(ai.shape, const2),
                  pl.BlockSpec((1, w_ssm), const2)],
        out_specs=pl.BlockSpec((1, lc, w_ssm), lambda b, c: (b, c, 0)),
        out_shape=jax.ShapeDtypeStruct((bsz, s, w_ssm), BF16),
        scratch_shapes=[pltpu.VMEM((n_tiles * pitch, LANES), F32),
                        pltpu.VMEM((n_tiles * pitch, LANES), F32),
                        pltpu.VMEM((2, n_tiles, LANES), F32)],
        compiler_params=pltpu.CompilerParams(
            dimension_semantics=("parallel", "arbitrary"),
            vmem_limit_bytes=_vmem_limit(est)),
        name="s5_scan",
    )(proj, bmat, cmat, ar, ai, d.reshape(1, w_ssm).astype(F32))


def _attn_tiles(s, tq, tk):
    tq, tk = min(tq, s), min(tk, s)
    assert s % tq == 0 and tq % (2 * tk) == 0
    return tq, tk


def _key_query_iota(tk, tq):
    key = lax.broadcasted_iota(jnp.int32, (tk, tq), 0)
    query = lax.broadcasted_iota(jnp.int32, (tk, tq), 1)
    return key, query


def _sb_kernel(q_ref, k_ref, v_ref, o_ref, z_ref, hi_ref, lo_ref, acc_ref, *, tq, tk):
    qi = pl.program_id(2)
    r = tq // tk
    q = q_ref[0]
    key, query = _key_query_iota(tk, tq)
    kk = lax.broadcasted_iota(jnp.int32, (tk, tk), 0)
    kk2 = lax.broadcasted_iota(jnp.int32, (tk, tk), 1)
    at_or_after = (kk2 >= kk).astype(BF16)

    def keep_logs(slot, kb, mask):
        k = k_ref[0, pl.ds(pl.multiple_of(kb * tk, tk), tk), :]
        z = lax.dot_general(k, q, _NT, preferred_element_type=F32)
        lk = -(jnp.maximum(z, 0.0) + jnp.log2(1.0 + jnp.exp2(-jnp.abs(z))))
        if mask is not None:
            lk = jnp.where(mask, lk, 0.0)
        hi = lk.astype(BF16)
        z_ref[slot] = z
        hi_ref[slot] = hi
        lo_ref[slot] = (lk - hi.astype(F32)).astype(BF16)

    def accumulate(slot, kb, carry, mask):
        v = v_ref[0, pl.ds(pl.multiple_of(kb * tk, tk), tk), :]
        incl = (jnp.dot(at_or_after, hi_ref[slot], preferred_element_type=F32)
                + jnp.dot(at_or_after, lo_ref[slot], preferred_element_type=F32))
        w = jnp.exp2(z_ref[slot] + incl + carry)
        if mask is not None:
            w = jnp.where(mask, w, 0.0)
        acc_ref[...] += lax.dot_general(v, w.astype(BF16), _TN, preferred_element_type=F32)
        return carry + incl[0:1, :]

    carry = jnp.zeros((1, tq), F32)
    acc_ref[...] = jnp.zeros_like(acc_ref)
    for d in reversed(range(r)):
        mask = (key + d * tk) < query
        keep_logs(d % 2, qi * r + d, mask)
        carry = accumulate(d % 2, qi * r + d, carry, mask)

    n_full = qi * r
    keep_logs(0, jnp.maximum(n_full - 1, 0), None)

    def body(it, carry):
        kb = n_full - 1 - 2 * it
        keep_logs(1, kb - 1, None)
        carry = accumulate(0, kb, carry, None)
        keep_logs(0, jnp.maximum(kb - 2, 0), None)
        return accumulate(1, kb - 1, carry, None)

    lax.fori_loop(0, n_full // 2, body, carry)
    o_ref[0] = acc_ref[...].T.astype(o_ref.dtype)


def _sb_attention(proj, *, heads, col0, tq, tk):
    bsz, s, _ = proj.shape
    tq, tk = _attn_tiles(s, tq, tk)
    assert col0 % HEAD_DIM == 0
    c0 = col0 // HEAD_DIM
    est = 2 * (2 * s * HEAD_DIM * 2) + 4 * tq * HEAD_DIM * 2 + 24 * tq * tk * 4
    return pl.pallas_call(
        functools.partial(_sb_kernel, tq=tq, tk=tk),
        grid=(bsz, heads, s // tq),
        in_specs=[pl.BlockSpec((1, tq, HEAD_DIM), lambda b, h, i: (b, i, c0 + h)),
                  pl.BlockSpec((1, s, HEAD_DIM), lambda b, h, i: (b, 0, c0 + heads + h)),
                  pl.BlockSpec((1, s, HEAD_DIM), lambda b, h, i: (b, 0, c0 + 2 * heads + h))],
        out_specs=pl.BlockSpec((1, tq, HEAD_DIM), lambda b, h, i: (b, i, h)),
        out_shape=jax.ShapeDtypeStruct((bsz, s, heads * HEAD_DIM), BF16),
        scratch_shapes=[pltpu.VMEM((2, tk, tq), F32), pltpu.VMEM((2, tk, tq), BF16),
                        pltpu.VMEM((2, tk, tq), BF16), pltpu.VMEM((HEAD_DIM, tq), F32)],
        compiler_params=pltpu.CompilerParams(
            dimension_semantics=("parallel", "parallel", "arbitrary"),
            vmem_limit_bytes=_vmem_limit(est)),
        name="stickbreak_attn",
    )(proj, proj, proj)


def _cum_kernel(f_ref, b_ref, o_ref, *, chunk):
    s = f_ref.shape[1]
    row = lax.broadcasted_iota(jnp.int32, (chunk, chunk), 0)
    col = lax.broadcasted_iota(jnp.int32, (chunk, chunk), 1)
    tri = (col <= row).astype(F32)
    carry = jnp.zeros((1, f_ref.shape[2]), F32)
    for c in range(s // chunk):
        ls = _log_sigmoid(f_ref[0, c * chunk:(c + 1) * chunk, :] + b_ref[...])
        cs = jnp.dot(tri, ls, preferred_element_type=F32, precision=lax.Precision.HIGHEST) + carry
        o_ref[0, c * chunk:(c + 1) * chunk, :] = cs * LOG2E
        carry = cs[chunk - 1:chunk, :]


def _forget_cumsum(f_logit, b_f, *, chunk=256):
    bsz, s, w = f_logit.shape
    chunk = min(chunk, s)
    assert s % chunk == 0
    blk = pl.BlockSpec((1, s, w), lambda b: (b, 0, 0))
    return pl.pallas_call(
        functools.partial(_cum_kernel, chunk=chunk),
        grid=(bsz,),
        in_specs=[blk, pl.BlockSpec((1, w), lambda b: (0, 0))],
        out_specs=blk,
        out_shape=jax.ShapeDtypeStruct((bsz, s, w), F32),
        compiler_params=pltpu.CompilerParams(dimension_semantics=("parallel",)),
        name="forget_cumsum",
    )(f_logit, b_f)


def _bf16_pieces(x, n):
    pieces = []
    for _ in range(n - 1):
        head = lax.bitcast_convert_type(
            lax.bitcast_convert_type(x, jnp.uint32) & jnp.uint32(0xFFFF0000), F32)
        pieces.append(head.astype(BF16))
        x = x - head
    pieces.append(x.astype(BF16))
    return pieces


def _augment_keys(k, cum):
    bsz, s, h = cum.shape
    extra = jnp.stack(_bf16_pieces(-cum, CUM_PIECES), axis=-1)
    extra = jnp.pad(extra, ((0, 0), (0, 0), (0, 0), (0, HEAD_DIM - CUM_PIECES)))
    return jnp.concatenate([k.reshape(bsz, s, h, HEAD_DIM), extra], axis=-1).reshape(bsz, s, h * MXU_DIM)


def _fox_kernel(q_ref, k_ref, v_ref, ci_ref, o_ref, z_ref, p_ref, acc_ref, *, tq, tk):
    qi = pl.program_id(2)
    r = tq // tk
    lane = lax.broadcasted_iota(jnp.int32, (tq, HEAD_DIM), 1)
    q = jnp.concatenate([q_ref[0], (lane < CUM_PIECES).astype(BF16)], axis=1)
    ci = ci_ref[0, 0, pl.ds(qi, 1), :]
    key, query = _key_query_iota(tk, tq)

    def logits(kb):
        k = k_ref[0, pl.ds(pl.multiple_of(kb * tk, tk), tk), :]
        return lax.dot_general(k, q, _NT, preferred_element_type=F32)

    def softmax(z, m, l):
        m_new = jnp.maximum(m, jnp.max(z, axis=0, keepdims=True) + ci)
        p = jnp.exp2(z - (m_new - ci))
        a = jnp.exp2(m - m_new)
        return p.astype(BF16), a, m_new, a * l + jnp.sum(p, axis=0, keepdims=True)

    def weighted(p, kb):
        v = v_ref[0, pl.ds(pl.multiple_of(kb * tk, tk), tk), :]
        return lax.dot_general(v, p, _TN, preferred_element_type=F32)

    m = jnp.full((1, tq), -jnp.inf, F32)
    l = jnp.zeros((1, tq), F32)
    acc_ref[...] = jnp.zeros_like(acc_ref)
    for d in range(r):
        z = jnp.where((key + d * tk) <= query, logits(qi * r + d), -jnp.inf)
        if d > 0:
            acc_ref[...] = a * acc_ref[...] + weighted(p_ref[(d - 1) % 2], qi * r + d - 1)
        p, a, m, l = softmax(z, m, l)
        p_ref[d % 2] = p

    n_full = qi * r
    z_ref[0] = logits(jnp.maximum(n_full - 1, 0))

    def step(slot, kb, kb_p, a, m, l):
        z_ref[1 - slot] = logits(jnp.maximum(kb - 1, 0))
        acc_ref[...] = a * acc_ref[...] + weighted(p_ref[1 - slot], kb_p)
        p, a, m, l = softmax(z_ref[slot], m, l)
        p_ref[slot] = p
        return a, m, l

    def body(it, state):
        a, m, l = state
        kb = n_full - 1 - 2 * it
        a, m, l = step(0, kb, jnp.where(it == 0, qi * r + r - 1, kb + 1), a, m, l)
        return step(1, kb - 1, kb, a, m, l)

    a, m, l = lax.fori_loop(0, n_full // 2, body, (a, m, l))
    acc = a * acc_ref[...] + weighted(p_ref[1], jnp.where(qi == 0, r - 1, 0))
    o_ref[0] = (acc / l).T.astype(o_ref.dtype)


def _fox_attention(qkv, k_aug, cum_t, *, heads, tq, tk):
    bsz, s, _ = qkv.shape
    tq, tk = _attn_tiles(s, tq, tk)
    nq = s // tq
    ci = cum_t.reshape(bsz, heads, nq, tq)
    est = 2 * (s * MXU_DIM * 2 + s * HEAD_DIM * 2) + 4 * tq * HEAD_DIM * 2 + 24 * tq * tk * 4
    return pl.pallas_call(
        functools.partial(_fox_kernel, tq=tq, tk=tk),
        grid=(bsz, heads, nq),
        in_specs=[pl.BlockSpec((1, tq, HEAD_DIM), lambda b, h, i: (b, i, h)),
                  pl.BlockSpec((1, s, MXU_DIM), lambda b, h, i: (b, 0, h)),
                  pl.BlockSpec((1, s, HEAD_DIM), lambda b, h, i: (b, 0, 2 * heads + h)),
                  pl.BlockSpec((1, 1, nq, tq), lambda b, h, i: (b, h, 0, 0))],
        out_specs=pl.BlockSpec((1, tq, HEAD_DIM), lambda b, h, i: (b, i, h)),
        out_shape=jax.ShapeDtypeStruct((bsz, s, heads * HEAD_DIM), BF16),
        scratch_shapes=[pltpu.VMEM((2, tk, tq), F32), pltpu.VMEM((2, tk, tq), BF16),
                        pltpu.VMEM((HEAD_DIM, tq), F32)],
        compiler_params=pltpu.CompilerParams(
            dimension_semantics=("parallel", "parallel", "arbitrary"),
            vmem_limit_bytes=_vmem_limit(est)),
        name="fox_attn",
    )(qkv, k_aug, qkv, ci)


ATTN_TQ = 512
ATTN_TK = 256


def _mlp_and_norms(h32, hb, mix_fn, ln_mix_g, ln_mix_b, w1, w2, ln_ffn_g, ln_ffn_b, *, alpha, tag):
    z = mix_fn(h32, hb)
    h32, hb = _layer_norm(z, ln_mix_g, ln_mix_b, tr=256, name=f"ln_mix_{tag}")
    hid = _matmul(hb, w1.astype(BF16), tm=1024, tn=512, tk=4096, epilogue="relu2", name=f"mlp_up_{tag}")
    z = _matmul(hid, w2.astype(BF16), tm=1024, tn=1024, tk=2048, epilogue="resid", extra=h32,
                alpha=alpha, out_dtype=F32, name=f"mlp_down_{tag}")
    return _layer_norm(z, ln_ffn_g, ln_ffn_b, tr=256, name=f"ln_ffn_{tag}")


def kernel(x, even_w_in, ssm_a_re, ssm_a_im, ssm_log_dt, ssm_b_re, ssm_b_im, ssm_c_re, ssm_c_im, ssm_d,
           ssm_w_glu, even_w_out, fox_w_in, fox_b_f, fox_w_out, ln_mix_g, ln_mix_b, mlp_w1, mlp_w2,
           ln_ffn_g, ln_ffn_b):
    bsz, s, d_model = x.shape
    depth = ln_mix_g.shape[0]
    alpha = (2.0 * depth) ** 0.25
    m = bsz * s
    w_ssm = ssm_d.shape[-1]

    h32 = x.reshape(m, d_model)
    hb = h32.astype(BF16)
    for layer in range(depth):
        i = layer // 2
        if layer % 2 == 0:
            def mix_fn(h32, hb, i=i):
                n_in = even_w_in.shape[-1]
                sb_heads = (n_in - w_ssm) // (3 * HEAD_DIM)
                proj = _matmul(hb, even_w_in[i].astype(BF16), tm=1024, tn=512, tk=4096, epilogue="colscale",
                               extra=_query_colscale(n_in, w_ssm, sb_heads * HEAD_DIM), name="even_in")
                proj3 = proj.reshape(bsz, s, -1)
                bmat, cmat, ar, ai = _ssm_discretise(ssm_a_re[i], ssm_a_im[i], ssm_log_dt[i], ssm_b_re[i],
                                                     ssm_b_im[i], ssm_c_re[i], ssm_c_im[i])
                g = _ssm_scan(proj3, bmat, cmat, ar, ai, ssm_d[i], w_ssm=w_ssm, lc=256).reshape(m, w_ssm)
                y_ssm = _matmul(g, ssm_w_glu[i].astype(BF16), tm=1024, tn=512, tk=2048, epilogue="glu",
                                extra=g, name="ssm_glu")
                y_sb = _sb_attention(proj3, heads=sb_heads, col0=w_ssm, tq=ATTN_TQ, tk=ATTN_TK).reshape(m, -1)
                y = jnp.concatenate([y_ssm, y_sb], axis=1)
                return _matmul(y, even_w_out[i].astype(BF16), tm=1024, tn=1024, tk=2048, epilogue="resid",
                               extra=h32, alpha=alpha, out_dtype=F32, name="even_out")
        else:
            def mix_fn(h32, hb, i=i):
                heads = fox_b_f.shape[-1]
                width = heads * HEAD_DIM
                w_in = fox_w_in[i]
                qkv = _matmul(hb, w_in[:, :3 * width].astype(BF16), tm=1024, tn=512, tk=4096,
                              epilogue="colscale", extra=_query_colscale(3 * width, 0, width), name="fox_in")
                qkv = qkv.reshape(bsz, s, 3 * width)
                w_f = jnp.pad(w_in[:, 3 * width:], ((0, 0), (0, LANES - heads))).astype(BF16)
                b_f = jnp.pad(fox_b_f[i].astype(F32), (0, LANES - heads)).reshape(1, LANES)
                f_logit = _matmul(hb, w_f, tm=1024, tn=LANES, tk=4096, out_dtype=F32, name="fox_forget")
                cum = _forget_cumsum(f_logit.reshape(bsz, s, LANES), b_f)[:, :, :heads]
                k_aug = _augment_keys(qkv[:, :, width:2 * width], cum)
                y = _fox_attention(qkv, k_aug, cum.transpose(0, 2, 1), heads=heads, tq=ATTN_TQ, tk=ATTN_TK)
                return _matmul(y.reshape(m, width), fox_w_out[i].astype(BF16), tm=1024, tn=1024, tk=2048,
                               epilogue="resid", extra=h32, alpha=alpha, out_dtype=F32, name="fox_out")
        h32, hb = _mlp_and_norms(h32, hb, mix_fn, ln_mix_g[layer], ln_mix_b[layer], mlp_w1[layer],
                                 mlp_w2[layer], ln_ffn_g[layer], ln_ffn_b[layer], alpha=alpha, tag=str(layer))
    return h32.reshape(bsz, s, d_model)
```

```python
import functools
import math

import jax
import jax.numpy as jnp
from jax import lax
from jax.experimental import pallas as pl
from jax.experimental.pallas import tpu as pltpu

F32 = jnp.float32
BF16 = jnp.bfloat16

HEAD_DIM = 128
LN_EPS = 1e-5
LOG2E = math.log2(math.e)
LANES = 128
SUBLANES = 8
MXU_DIM = 256
VMEM_CAP = 56 * 1024 * 1024
CUM_PIECES = 3
EXP2_CLAMP = 126.0

_NT = (((1,), (1,)), ((), ()))
_TN = (((0,), (0,)), ((), ()))


def _vmem_limit(estimate_bytes):
    return int(min(VMEM_CAP, max(32 * 1024 * 1024, estimate_bytes * 5 // 4)))


def _log_sigmoid(x):
    return jnp.minimum(x, 0.0) - jnp.log(1.0 + jnp.exp(-jnp.abs(x)))


def _gelu_tanh(x):
    c = math.sqrt(2.0 / math.pi)
    return 0.5 * x * (1.0 + jnp.tanh(c * (x + 0.044715 * (x * x * x))))


def _epilogue(acc, e_ref, o_ref, epilogue, alpha):
    if epilogue == "cast":
        o_ref[...] = acc.astype(o_ref.dtype)
    elif epilogue == "colscale":
        o_ref[...] = (acc * e_ref[...]).astype(o_ref.dtype)
    elif epilogue == "relu2":
        r = jnp.maximum(acc, 0.0)
        o_ref[...] = (r * r).astype(o_ref.dtype)
    elif epilogue == "glu":
        g = e_ref[...].astype(F32)
        o_ref[...] = (g * (1.0 / (1.0 + jnp.exp(-acc)))).astype(o_ref.dtype)
    elif epilogue == "resid":
        o_ref[...] = (alpha * e_ref[...].astype(F32) + acc).astype(o_ref.dtype)
    else:
        raise ValueError(epilogue)


def _mm_kernel(*refs, nk, epilogue, alpha, has_extra):
    if has_extra:
        a_ref, b_ref, e_ref, o_ref = refs[:4]
        rest = refs[4:]
    else:
        a_ref, b_ref, o_ref = refs[:3]
        e_ref = None
        rest = refs[3:]

    if nk == 1:
        _epilogue(jnp.dot(a_ref[...], b_ref[...], preferred_element_type=F32), e_ref, o_ref, epilogue, alpha)
        return

    acc_ref = rest[0]
    k = pl.program_id(2)

    @pl.when(k == 0)
    def _():
        acc_ref[...] = jnp.zeros_like(acc_ref)

    acc_ref[...] += jnp.dot(a_ref[...], b_ref[...], preferred_element_type=F32)

    @pl.when(k == nk - 1)
    def _():
        _epilogue(acc_ref[...], e_ref, o_ref, epilogue, alpha)


def _mm_w32_kernel(*refs, n_a, epilogue, alpha, has_extra):
    a_refs = refs[:n_a]
    b_ref = refs[n_a]
    e_ref = refs[n_a + 1] if has_extra else None
    o_ref, wb_ref = refs[-2], refs[-1]

    @pl.when(pl.program_id(1) == 0)
    def _():
        wb_ref[...] = b_ref[...].astype(BF16)

    acc = None
    k0 = 0
    for a_ref in a_refs:
        kw = a_ref.shape[1]
        part = jnp.dot(a_ref[...], wb_ref[k0:k0 + kw, :], preferred_element_type=F32)
        acc = part if acc is None else acc + part
        k0 += kw
    _epilogue(acc, e_ref, o_ref, epilogue, alpha)


def _matmul_w32(a_parts, b, *, tm, tn, col_blocks=None, epilogue="cast", extra=None, alpha=1.0,
                out_dtype=BF16, name):
    a_parts = tuple(a_parts)
    m = a_parts[0].shape[0]
    kdim, n_all = b.shape
    assert sum(a.shape[1] for a in a_parts) == kdim
    tm, tn = min(tm, m), min(tn, n_all)
    nj = n_all // tn if col_blocks is None else col_blocks
    n = nj * tn
    assert m % tm == 0 and n <= n_all
    has_extra = extra is not None
    in_specs = [pl.BlockSpec((tm, a.shape[1]), lambda j, i: (i, 0)) for a in a_parts]
    in_specs.append(pl.BlockSpec((kdim, tn), lambda j, i: (0, j)))
    args = list(a_parts) + [b]
    est = 2 * tm * kdim * 2 + 2 * kdim * tn * 4 + kdim * tn * 2
    est += 2 * tm * tn * jnp.dtype(out_dtype).itemsize + 2 * tm * tn * 4
    if has_extra:
        if epilogue == "colscale":
            in_specs.append(pl.BlockSpec((1, tn), lambda j, i: (0, j)))
        else:
            in_specs.append(pl.BlockSpec((tm, tn), lambda j, i: (i, j)))
            est += 2 * tm * tn * extra.dtype.itemsize
        args.append(extra)
    return pl.pallas_call(
        functools.partial(_mm_w32_kernel, n_a=len(a_parts), epilogue=epilogue, alpha=alpha,
                          has_extra=has_extra),
        grid=(nj, m // tm),
        in_specs=in_specs,
        out_specs=pl.BlockSpec((tm, tn), lambda j, i: (i, j)),
        out_shape=jax.ShapeDtypeStruct((m, n), out_dtype),
        scratch_shapes=[pltpu.VMEM((kdim, tn), BF16)],
        compiler_params=pltpu.CompilerParams(
            dimension_semantics=("parallel", "arbitrary"),
            vmem_limit_bytes=_vmem_limit(est)),
        name=name,
    )(*args)


def _matmul(a, b, *, tm, tn, tk, epilogue="cast", extra=None, alpha=1.0, out_dtype=BF16, name):
    m, kdim = a.shape
    k2, n = b.shape
    assert kdim == k2
    tm, tn, tk = min(tm, m), min(tn, n), min(tk, kdim)
    assert m % tm == 0 and n % tn == 0 and kdim % tk == 0, (a.shape, b.shape, tm, tn, tk)
    nk = kdim // tk
    has_extra = extra is not None
    in_specs = [pl.BlockSpec((tm, tk), lambda i, j, k: (i, k)),
                pl.BlockSpec((tk, tn), lambda i, j, k: (k, j))]
    args = [a, b]
    est = 2 * (tm * tk * a.dtype.itemsize + tk * tn * b.dtype.itemsize)
    est += 2 * tm * tn * jnp.dtype(out_dtype).itemsize + 2 * tm * tn * 4
    if has_extra:
        if epilogue == "colscale":
            in_specs.append(pl.BlockSpec((1, tn), lambda i, j, k: (0, j)))
        else:
            in_specs.append(pl.BlockSpec((tm, tn), lambda i, j, k: (i, j)))
            est += 2 * tm * tn * extra.dtype.itemsize
        args.append(extra)
    scratch = [pltpu.VMEM((tm, tn), F32)] if nk > 1 else []
    return pl.pallas_call(
        functools.partial(_mm_kernel, nk=nk, epilogue=epilogue, alpha=alpha, has_extra=has_extra),
        grid=(m // tm, n // tn, nk),
        in_specs=in_specs,
        out_specs=pl.BlockSpec((tm, tn), lambda i, j, k: (i, j)),
        out_shape=jax.ShapeDtypeStruct((m, n), out_dtype),
        scratch_shapes=scratch,
        compiler_params=pltpu.CompilerParams(
            dimension_semantics=("parallel", "parallel", "arbitrary"),
            vmem_limit_bytes=_vmem_limit(est)),
        name=name,
    )(*args)


def _query_colscale(n_cols, q_start, q_width):
    col = jnp.arange(n_cols)
    is_q = (col >= q_start) & (col < q_start + q_width)
    return jnp.where(is_q, HEAD_DIM ** -0.5 * LOG2E, 1.0).astype(F32).reshape(1, n_cols)


def _ln_kernel(x_ref, g_ref, b_ref, of_ref, ob_ref):
    x = x_ref[...]
    mu = jnp.mean(x, axis=-1, keepdims=True)
    xc = x - mu
    var = jnp.mean(xc * xc, axis=-1, keepdims=True)
    y = xc * lax.rsqrt(var + LN_EPS) * g_ref[...] + b_ref[...]
    of_ref[...] = y
    ob_ref[...] = y.astype(BF16)


def _layer_norm(x, g, b, *, tr, name):
    m, d = x.shape
    tr = min(tr, m)
    assert m % tr == 0
    row = pl.BlockSpec((tr, d), lambda i: (i, 0))
    vec = pl.BlockSpec((1, d), lambda i: (0, 0))
    return pl.pallas_call(
        _ln_kernel,
        grid=(m // tr,),
        in_specs=[row, vec, vec],
        out_specs=[row, row],
        out_shape=[jax.ShapeDtypeStruct((m, d), F32), jax.ShapeDtypeStruct((m, d), BF16)],
        compiler_params=pltpu.CompilerParams(
            dimension_semantics=("parallel",),
            vmem_limit_bytes=_vmem_limit(2 * tr * d * (4 + 4 + 2) + 4 * tr * d * 4)),
        name=name,
    )(x, g.reshape(1, d).astype(F32), b.reshape(1, d).astype(F32))


def _ssm_kernel(u_ref, bm_ref, cm_ref, ar_ref, ai_ref, d_ref, o_ref, sre, sim, st_ref,
                *, lc, pitch, nq, tiles_per_q, n_vregs):
    c = pl.program_id(1)

    @pl.when(c == 0)
    def _():
        st_ref[...] = jnp.zeros_like(st_ref)

    for q in range(nq):
        uq = u_ref[0, :, q * LANES:(q + 1) * LANES]
        res = jnp.dot(uq, bm_ref[q], preferred_element_type=F32)
        for tl in range(tiles_per_q):
            base = (q * tiles_per_q + tl) * pitch
            sre[base:base + lc, :] = res[:, (2 * tl) * LANES:(2 * tl + 1) * LANES]
            sim[base:base + lc, :] = res[:, (2 * tl + 1) * LANES:(2 * tl + 2) * LANES]

    ar = [ar_ref[j * SUBLANES:(j + 1) * SUBLANES, :] for j in range(n_vregs)]
    ai = [ai_ref[j * SUBLANES:(j + 1) * SUBLANES, :] for j in range(n_vregs)]
    xr0 = tuple(st_ref[0, j * SUBLANES:(j + 1) * SUBLANES, :] for j in range(n_vregs))
    xi0 = tuple(st_ref[1, j * SUBLANES:(j + 1) * SUBLANES, :] for j in range(n_vregs))

    def step(t, carry):
        xr, xi = carry
        nxr, nxi = [], []
        for j in range(n_vregs):
            rows = pl.ds(j * SUBLANES * pitch + t, SUBLANES, stride=pitch)
            nr = ar[j] * xr[j] - ai[j] * xi[j] + sre[rows, :]
            ni = ar[j] * xi[j] + ai[j] * xr[j] + sim[rows, :]
            sre[rows, :] = nr
            sim[rows, :] = ni
            nxr.append(nr)
            nxi.append(ni)
        return tuple(nxr), tuple(nxi)

    xr, xi = lax.fori_loop(0, lc, step, (xr0, xi0))
    for j in range(n_vregs):
        st_ref[0, j * SUBLANES:(j + 1) * SUBLANES, :] = xr[j]
        st_ref[1, j * SUBLANES:(j + 1) * SUBLANES, :] = xi[j]

    for q in range(nq):
        pieces = []
        for tl in range(tiles_per_q):
            base = (q * tiles_per_q + tl) * pitch
            pieces.append(sre[base:base + lc, :].astype(BF16))
            pieces.append(sim[base:base + lc, :].astype(BF16))
        xcat = jnp.concatenate(pieces, axis=1)
        y = jnp.dot(xcat, cm_ref[q], preferred_element_type=F32)
        uq = u_ref[0, :, q * LANES:(q + 1) * LANES].astype(F32)
        y = y + d_ref[:, q * LANES:(q + 1) * LANES] * uq
        o_ref[0, :, q * LANES:(q + 1) * LANES] = _gelu_tanh(y).astype(o_ref.dtype)


def _ssm_discretise(a_re, a_im, log_dt, b_re, b_im, c_re, c_im):
    g, n = a_re.shape
    p = b_re.shape[-1]
    lr = a_re.astype(F32)
    li = a_im.astype(F32)
    dt = jnp.exp(log_dt.astype(F32))[:, None]
    mag = jnp.exp(lr * dt)
    abar_r = mag * jnp.cos(li * dt)
    abar_i = mag * jnp.sin(li * dt)
    den = lr * lr + li * li
    nr = abar_r - 1.0
    ni = abar_i
    zr = (nr * lr + ni * li) / den
    zi = (ni * lr - nr * li) / den
    br = b_re.astype(F32)
    bi = b_im.astype(F32)
    bbar_r = zr[..., None] * br - zi[..., None] * bi
    bbar_i = zr[..., None] * bi + zi[..., None] * br

    gq = LANES // p
    gt = LANES // n
    nq = g // gq
    tq = gq // gt
    eye = jnp.eye(gq, dtype=F32)
    bb = jnp.stack([bbar_r, bbar_i], 0).reshape(2, nq, gq, n, p)
    bf = jnp.einsum("aqhnp,gh->qgpahn", bb, eye)
    bf = bf.reshape(nq, gq, p, 2, tq, gt, n).transpose(0, 1, 2, 4, 3, 5, 6)
    bmat = bf.reshape(nq, gq * p, tq * 2 * gt * n).astype(BF16)
    cc = jnp.stack([c_re.astype(F32), -c_im.astype(F32)], 0).reshape(2, nq, gq, p, n)
    cf = jnp.einsum("aqhpn,gh->qahngp", cc, eye)
    cf = cf.reshape(nq, 2, tq, gt, n, gq, p).transpose(0, 2, 1, 3, 4, 5, 6)
    cmat = cf.reshape(nq, tq * 2 * gt * n, gq * p).astype(BF16)
    n_tiles = g * n // LANES
    return bmat, cmat, abar_r.reshape(n_tiles, LANES), abar_i.reshape(n_tiles, LANES)


def _ssm_scan(proj, bmat, cmat, ar, ai, d, *, w_ssm, lc):
    bsz, s, _ = proj.shape
    lc = min(lc, s)
    assert s % lc == 0 and lc % SUBLANES == 0
    nq, _, qcols = bmat.shape
    tiles_per_q = qcols // (2 * LANES)
    n_tiles = ar.shape[0]
    assert n_tiles == nq * tiles_per_q and n_tiles % SUBLANES == 0 and w_ssm == nq * LANES
    n_vregs = n_tiles // SUBLANES
    pitch = lc + 4
    est = (2 * 2 * lc * w_ssm * 2 + 2 * (bmat.size + cmat.size) * 2
           + 2 * n_tiles * pitch * LANES * 4 + 4 * lc * qcols * 4)
    kern = functools.partial(_ssm_kernel, lc=lc, pitch=pitch, nq=nq, tiles_per_q=tiles_per_q,
                             n_vregs=n_vregs)
    const2 = lambda b, c: (0, 0)
    const3 = lambda b, c: (0, 0, 0)
    return pl.pallas_call(
        kern,
        grid=(bsz, s // lc),
        in_specs=[pl.BlockSpec((1, lc, w_ssm), lambda b, c: (b, c, 0)),
                  pl.BlockSpec(bmat.shape, const3),
                  pl.BlockSpec(cmat.shape, const3),
                  pl.BlockSpec(ar.shape, const2),
                  pl.BlockSpec(ai.shape, const2),
                  pl.BlockSpec((1, w_ssm), const2)],
        out_specs=pl.BlockSpec((1, lc, w_ssm), lambda b, c: (b, c, 0)),
        out_shape=jax.ShapeDtypeStruct((bsz, s, w_ssm), BF16),
        scratch_shapes=[pltpu.VMEM((n_tiles * pitch, LANES), F32),
                        pltpu.VMEM((n_tiles * pitch, LANES), F32),
                        pltpu.VMEM((2, n_tiles, LANES), F32)],
        compiler_params=pltpu.CompilerParams(
            dimension_semantics=("parallel", "arbitrary"),
            vmem_limit_bytes=_vmem_limit(est)),
        name="s5_scan",
    )(proj, bmat, cmat, ar, ai, d.reshape(1, w_ssm).astype(F32))


def _attn_tiles(s, tq, tk):
    tq, tk = min(tq, s), min(tk, s)
    assert s % tq == 0 and tq % (2 * tk) == 0
    return tq, tk


def _key_query_iota(tk, tq):
    key = lax.broadcasted_iota(jnp.int32, (tk, tq), 0)
    query = lax.broadcasted_iota(jnp.int32, (tk, tq), 1)
    return key, query


def _transpose_values(v_ref, vt_ref, hs, chunk=512):
    s = v_ref.shape[1]
    chunk = min(chunk, s)

    @pl.when(pl.program_id(2) == 0)
    def _():
        for h in range(hs):
            for c in range(s // chunk):
                rows = slice(c * chunk, (c + 1) * chunk)
                vt_ref[h, :, rows] = v_ref[0, rows, h * HEAD_DIM:(h + 1) * HEAD_DIM].T


def _sb_kernel(q_ref, k_ref, v_ref, o_ref, z_ref, hi_ref, lo_ref, acc_ref, vt_ref, *, tq, tk, hs):
    qi = pl.program_id(2)
    r = tq // tk
    heads = range(hs)
    key, query = _key_query_iota(tk, tq)
    kk = lax.broadcasted_iota(jnp.int32, (tk, tk), 0)
    kk2 = lax.broadcasted_iota(jnp.int32, (tk, tk), 1)
    at_or_after = (kk2 >= kk).astype(BF16)

    def head_cols(h):
        return slice(h * HEAD_DIM, (h + 1) * HEAD_DIM)

    q = [q_ref[0, :, head_cols(h)] for h in heads]
    _transpose_values(v_ref, vt_ref, hs)

    def keep_logs(h, slot, kb, mask):
        k = k_ref[0, pl.ds(pl.multiple_of(kb * tk, tk), tk), head_cols(h)]
        z = lax.dot_general(k, q[h], _NT, preferred_element_type=F32)
        nk = jnp.maximum(z, jnp.log2(1.0 + jnp.exp2(jnp.minimum(z, EXP2_CLAMP))))
        if mask is not None:
            nk = jnp.where(mask, nk, 0.0)
        hi = nk.astype(BF16)
        z_ref[h, slot] = z
        hi_ref[h, slot] = hi
        lo_ref[h, slot] = (nk - hi.astype(F32)).astype(BF16)

    def accumulate(h, slot, kb, carry, mask):
        vt = vt_ref[h, :, pl.ds(pl.multiple_of(kb * tk, tk), tk)]
        incl = (jnp.dot(at_or_after, hi_ref[h, slot], preferred_element_type=F32)
                + jnp.dot(at_or_after, lo_ref[h, slot], preferred_element_type=F32))
        w = jnp.exp2(z_ref[h, slot] - incl - carry)
        if mask is not None:
            w = jnp.where(mask, w, 0.0)
        acc_ref[h] += jnp.dot(vt, w.astype(BF16), preferred_element_type=F32)
        return carry + incl[0:1, :]

    carry = [jnp.zeros((1, tq), F32) for _ in heads]
    acc_ref[...] = jnp.zeros_like(acc_ref)
    for d in reversed(range(r)):
        mask = (key + d * tk) < query
        for h in heads:
            keep_logs(h, d % 2, qi * r + d, mask)
        carry = [accumulate(h, d % 2, qi * r + d, carry[h], mask) for h in heads]

    n_full = qi * r
    for h in heads:
        keep_logs(h, 0, jnp.maximum(n_full - 1, 0), None)

    def body(it, carry):
        kb = n_full - 1 - 2 * it
        for h in heads:
            keep_logs(h, 1, kb - 1, None)
        carry = [accumulate(h, 0, kb, carry[h], None) for h in heads]
        for h in heads:
            keep_logs(h, 0, jnp.maximum(kb - 2, 0), None)
        return [accumulate(h, 1, kb - 1, carry[h], None) for h in heads]

    lax.fori_loop(0, n_full // 2, body, carry)
    for h in heads:
        o_ref[0, :, head_cols(h)] = acc_ref[h].T.astype(o_ref.dtype)


def _sb_attention(proj, *, heads, col0, tq, tk):
    bsz, s, _ = proj.shape
    tq, tk = _attn_tiles(s, tq, tk)
    hs = ATTN_HEADS_PER_STEP
    wcols = hs * HEAD_DIM
    assert heads % hs == 0 and col0 % wcols == 0 and (heads * HEAD_DIM) % wcols == 0
    c0, hg = col0 // wcols, heads // hs
    est = 2 * (2 * s * wcols * 2) + 4 * tq * wcols * 2 + hs * 16 * tq * tk * 4
    return pl.pallas_call(
        functools.partial(_sb_kernel, tq=tq, tk=tk, hs=hs),
        grid=(bsz, hg, s // tq),
        in_specs=[pl.BlockSpec((1, tq, wcols), lambda b, h, i: (b, i, c0 + h)),
                  pl.BlockSpec((1, s, wcols), lambda b, h, i: (b, 0, c0 + hg + h)),
                  pl.BlockSpec((1, s, wcols), lambda b, h, i: (b, 0, c0 + 2 * hg + h))],
        out_specs=pl.BlockSpec((1, tq, wcols), lambda b, h, i: (b, i, h)),
        out_shape=jax.ShapeDtypeStruct((bsz, s, heads * HEAD_DIM), BF16),
        scratch_shapes=[pltpu.VMEM((hs, 2, tk, tq), F32), pltpu.VMEM((hs, 2, tk, tq), BF16),
                        pltpu.VMEM((hs, 2, tk, tq), BF16), pltpu.VMEM((hs, HEAD_DIM, tq), F32),
                        pltpu.VMEM((hs, HEAD_DIM, s), BF16)],
        compiler_params=pltpu.CompilerParams(
            dimension_semantics=("parallel", "parallel", "arbitrary"),
            vmem_limit_bytes=_vmem_limit(est)),
        name="stickbreak_attn",
    )(proj, proj, proj)


def _cum_kernel(f_ref, b_ref, o_ref, *, chunk):
    s = f_ref.shape[1]
    row = lax.broadcasted_iota(jnp.int32, (chunk, chunk), 0)
    col = lax.broadcasted_iota(jnp.int32, (chunk, chunk), 1)
    tri = (col <= row).astype(F32)
    carry = jnp.zeros((1, f_ref.shape[2]), F32)
    for c in range(s // chunk):
        ls = _log_sigmoid(f_ref[0, c * chunk:(c + 1) * chunk, :] + b_ref[...])
        cs = jnp.dot(tri, ls, preferred_element_type=F32, precision=lax.Precision.HIGHEST) + carry
        o_ref[0, c * chunk:(c + 1) * chunk, :] = cs * LOG2E
        carry = cs[chunk - 1:chunk, :]


def _forget_cumsum(f_logit, b_f, *, chunk=256):
    bsz, s, w = f_logit.shape
    chunk = min(chunk, s)
    assert s % chunk == 0
    blk = pl.BlockSpec((1, s, w), lambda b: (b, 0, 0))
    return pl.pallas_call(
        functools.partial(_cum_kernel, chunk=chunk),
        grid=(bsz,),
        in_specs=[blk, pl.BlockSpec((1, w), lambda b: (0, 0))],
        out_specs=blk,
        out_shape=jax.ShapeDtypeStruct((bsz, s, w), F32),
        compiler_params=pltpu.CompilerParams(dimension_semantics=("parallel",)),
        name="forget_cumsum",
    )(f_logit, b_f)


def _bf16_pieces(x, n):
    pieces = []
    for _ in range(n - 1):
        head = lax.bitcast_convert_type(
            lax.bitcast_convert_type(x, jnp.uint32) & jnp.uint32(0xFFFF0000), F32)
        pieces.append(head.astype(BF16))
        x = x - head
    pieces.append(x.astype(BF16))
    return pieces


def _key_cum_columns(cum):
    bsz, s, h = cum.shape
    extra = jnp.stack(_bf16_pieces(-cum, CUM_PIECES), axis=-1)
    extra = jnp.pad(extra, ((0, 0), (0, 0), (0, 0), (0, HEAD_DIM - CUM_PIECES)))
    return extra.reshape(bsz, s, h * HEAD_DIM)


def _fox_kernel(q_ref, k_ref, kc_ref, v_ref, ci_ref, o_ref, z_ref, p_ref, acc_ref, vt_ref, *, tq, tk, hs):
    qi = pl.program_id(2)
    r = tq // tk
    heads = range(hs)
    lane = lax.broadcasted_iota(jnp.int32, (tq, HEAD_DIM), 1)
    ones = (lane < CUM_PIECES).astype(BF16)
    key, query = _key_query_iota(tk, tq)

    def head_cols(h):
        return slice(h * HEAD_DIM, (h + 1) * HEAD_DIM)

    q = [jnp.concatenate([q_ref[0, :, head_cols(h)], ones], axis=1) for h in heads]
    _transpose_values(v_ref, vt_ref, hs)
    ci = [ci_ref[0, h, pl.ds(qi, 1), :] for h in heads]

    def logits(h, kb):
        rows = pl.ds(pl.multiple_of(kb * tk, tk), tk)
        k = jnp.concatenate([k_ref[0, rows, head_cols(h)], kc_ref[0, rows, head_cols(h)]], axis=1)
        return lax.dot_general(k, q[h], _NT, preferred_element_type=F32)

    def softmax(h, z, st):
        _, m, l = st
        m_new = jnp.maximum(m, jnp.max(z, axis=0, keepdims=True) + ci[h])
        p = jnp.exp2(z - (m_new - ci[h]))
        a = jnp.exp2(m - m_new)
        return p.astype(BF16), (a, m_new, a * l + jnp.sum(p, axis=0, keepdims=True))

    def rescale_add(h, st, p, kb):
        vt = vt_ref[h, :, pl.ds(pl.multiple_of(kb * tk, tk), tk)]
        return st[0] * acc_ref[h] + jnp.dot(vt, p, preferred_element_type=F32)

    st = [(None, jnp.full((1, tq), -jnp.inf, F32), jnp.zeros((1, tq), F32)) for _ in heads]
    acc_ref[...] = jnp.zeros_like(acc_ref)
    for d in range(r):
        mask = (key + d * tk) <= query
        z = [jnp.where(mask, logits(h, qi * r + d), -jnp.inf) for h in heads]
        if d > 0:
            for h in heads:
                acc_ref[h] = rescale_add(h, st[h], p_ref[h, (d - 1) % 2], qi * r + d - 1)
        for h in heads:
            p_ref[h, d % 2], st[h] = softmax(h, z[h], st[h])

    n_full = qi * r
    for h in heads:
        z_ref[h, 0] = logits(h, jnp.maximum(n_full - 1, 0))

    def step(slot, kb, kb_p, st):
        st = list(st)
        for h in heads:
            z_ref[h, 1 - slot] = logits(h, jnp.maximum(kb - 1, 0))
        for h in heads:
            acc_ref[h] = rescale_add(h, st[h], p_ref[h, 1 - slot], kb_p)
        for h in heads:
            p_ref[h, slot], st[h] = softmax(h, z_ref[h, slot], st[h])
        return st

    def body(it, st):
        kb = n_full - 1 - 2 * it
        st = step(0, kb, jnp.where(it == 0, qi * r + r - 1, kb + 1), st)
        return step(1, kb - 1, kb, st)

    st = lax.fori_loop(0, n_full // 2, body, st)
    for h in heads:
        acc = rescale_add(h, st[h], p_ref[h, 1], jnp.where(qi == 0, r - 1, 0))
        o_ref[0, :, head_cols(h)] = (acc / st[h][2]).T.astype(o_ref.dtype)


def _fox_attention(qkv, kc, cum_t, *, heads, tq, tk):
    bsz, s, _ = qkv.shape
    tq, tk = _attn_tiles(s, tq, tk)
    nq = s // tq
    ci = cum_t.reshape(bsz, heads, nq, tq)
    hs = ATTN_HEADS_PER_STEP
    wcols = hs * HEAD_DIM
    assert heads % hs == 0
    hg = heads // hs
    est = 2 * (3 * s * wcols * 2) + 4 * tq * wcols * 2 + hs * 16 * tq * tk * 4
    return pl.pallas_call(
        functools.partial(_fox_kernel, tq=tq, tk=tk, hs=hs),
        grid=(bsz, hg, nq),
        in_specs=[pl.BlockSpec((1, tq, wcols), lambda b, h, i: (b, i, h)),
                  pl.BlockSpec((1, s, wcols), lambda b, h, i: (b, 0, hg + h)),
                  pl.BlockSpec((1, s, wcols), lambda b, h, i: (b, 0, h)),
                  pl.BlockSpec((1, s, wcols), lambda b, h, i: (b, 0, 2 * hg + h)),
                  pl.BlockSpec((1, hs, nq, tq), lambda b, h, i: (b, h, 0, 0))],
        out_specs=pl.BlockSpec((1, tq, wcols), lambda b, h, i: (b, i, h)),
        out_shape=jax.ShapeDtypeStruct((bsz, s, heads * HEAD_DIM), BF16),
        scratch_shapes=[pltpu.VMEM((hs, 2, tk, tq), F32), pltpu.VMEM((hs, 2, tk, tq), BF16),
                        pltpu.VMEM((hs, HEAD_DIM, tq), F32), pltpu.VMEM((hs, HEAD_DIM, s), BF16)],
        compiler_params=pltpu.CompilerParams(
            dimension_semantics=("parallel", "parallel", "arbitrary"),
            vmem_limit_bytes=_vmem_limit(est)),
        name="fox_attn",
    )(qkv, qkv, kc, qkv, ci)


ATTN_TQ = 512
ATTN_TK = 256
ATTN_HEADS_PER_STEP = 2


def _mlp_and_norms(h32, hb, mix_fn, ln_mix_g, ln_mix_b, w1, w2, ln_ffn_g, ln_ffn_b, *, alpha, tag):
    z = mix_fn(h32, hb)
    h32, hb = _layer_norm(z, ln_mix_g, ln_mix_b, tr=256, name=f"ln_mix_{tag}")
    hid = _matmul_w32([hb], w1, tm=1024, tn=512, epilogue="relu2", name=f"mlp_up_{tag}")
    z = _matmul(hid, w2.astype(BF16), tm=1024, tn=1024, tk=2048, epilogue="resid", extra=h32,
                alpha=alpha, out_dtype=F32, name=f"mlp_down_{tag}")
    return _layer_norm(z, ln_ffn_g, ln_ffn_b, tr=256, name=f"ln_ffn_{tag}")


def kernel(x, even_w_in, ssm_a_re, ssm_a_im, ssm_log_dt, ssm_b_re, ssm_b_im, ssm_c_re, ssm_c_im, ssm_d,
           ssm_w_glu, even_w_out, fox_w_in, fox_b_f, fox_w_out, ln_mix_g, ln_mix_b, mlp_w1, mlp_w2,
           ln_ffn_g, ln_ffn_b):
    bsz, s, d_model = x.shape
    depth = ln_mix_g.shape[0]
    alpha = (2.0 * depth) ** 0.25
    m = bsz * s
    w_ssm = ssm_d.shape[-1]

    h32 = x.reshape(m, d_model)
    hb = h32.astype(BF16)
    for layer in range(depth):
        i = layer // 2
        if layer % 2 == 0:
            def mix_fn(h32, hb, i=i):
                n_in = even_w_in.shape[-1]
                sb_heads = (n_in - w_ssm) // (3 * HEAD_DIM)
                proj = _matmul_w32([hb], even_w_in[i], tm=1024, tn=512, epilogue="colscale",
                                   extra=_query_colscale(n_in, w_ssm, sb_heads * HEAD_DIM), name="even_in")
                proj3 = proj.reshape(bsz, s, -1)
                bmat, cmat, ar, ai = _ssm_discretise(ssm_a_re[i], ssm_a_im[i], ssm_log_dt[i], ssm_b_re[i],
                                                     ssm_b_im[i], ssm_c_re[i], ssm_c_im[i])
                g = _ssm_scan(proj3, bmat, cmat, ar, ai, ssm_d[i], w_ssm=w_ssm, lc=256).reshape(m, w_ssm)
                y_ssm = _matmul_w32([g], ssm_w_glu[i], tm=1024, tn=512, epilogue="glu", extra=g, name="ssm_glu")
                y_sb = _sb_attention(proj3, heads=sb_heads, col0=w_ssm, tq=ATTN_TQ, tk=ATTN_TK).reshape(m, -1)
                return _matmul_w32([y_ssm, y_sb], even_w_out[i], tm=1024, tn=512, epilogue="resid",
                                   extra=h32, alpha=alpha, out_dtype=F32, name="even_out")
        else:
            def mix_fn(h32, hb, i=i):
                heads = fox_b_f.shape[-1]
                width = heads * HEAD_DIM
                w_in = fox_w_in[i]
                qkv = _matmul_w32([hb], w_in, tm=1024, tn=512, col_blocks=3 * width // 512, epilogue="colscale",
                                  extra=_query_colscale(3 * width, 0, width), name="fox_in")
                qkv = qkv.reshape(bsz, s, 3 * width)
                w_f = jnp.pad(w_in[:, 3 * width:], ((0, 0), (0, LANES - heads))).astype(BF16)
                b_f = jnp.pad(fox_b_f[i].astype(F32), (0, LANES - heads)).reshape(1, LANES)
                f_logit = _matmul(hb, w_f, tm=1024, tn=LANES, tk=4096, out_dtype=F32, name="fox_forget")
                cum = _forget_cumsum(f_logit.reshape(bsz, s, LANES), b_f)[:, :, :heads]
                y = _fox_attention(qkv, _key_cum_columns(cum), cum.transpose(0, 2, 1), heads=heads,
                                   tq=ATTN_TQ, tk=ATTN_TK)
                return _matmul_w32([y.reshape(m, width)], fox_w_out[i], tm=1024, tn=512, epilogue="resid",
                                   extra=h32, alpha=alpha, out_dtype=F32, name="fox_out")
        h32, hb = _mlp_and_norms(h32, hb, mix_fn, ln_mix_g[layer], ln_mix_b[layer], mlp_w1[layer],
                                 mlp_w2[layer], ln_ffn_g[layer], ln_ffn_b[layer], alpha=alpha, tag=str(layer))
    return h32.reshape(bsz, s, d_model)
```

```python
import functools
import math

import jax
import jax.numpy as jnp
from jax import lax
from jax.experimental import pallas as pl
from jax.experimental.pallas import tpu as pltpu

F32 = jnp.float32
BF16 = jnp.bfloat16

HEAD_DIM = 128
LN_EPS = 1e-5
LOG2E = math.log2(math.e)
LANES = 128
SUBLANES = 8
MXU_DIM = 256
VMEM_CAP = 56 * 1024 * 1024
CUM_PIECES = 3
EXP2_CLAMP = 126.0

_NT = (((1,), (1,)), ((), ()))
_TN = (((0,), (0,)), ((), ()))


def _vmem_limit(estimate_bytes):
    return int(min(VMEM_CAP, max(32 * 1024 * 1024, estimate_bytes * 5 // 4)))


def _log_sigmoid(x):
    return jnp.minimum(x, 0.0) - jnp.log(1.0 + jnp.exp(-jnp.abs(x)))


def _gelu_tanh(x):
    c = math.sqrt(2.0 / math.pi)
    return 0.5 * x * (1.0 + jnp.tanh(c * (x + 0.044715 * (x * x * x))))


def _epilogue(acc, e_ref, o_ref, epilogue, alpha):
    if epilogue == "cast":
        o_ref[...] = acc.astype(o_ref.dtype)
    elif epilogue == "colscale":
        o_ref[...] = (acc * e_ref[...]).astype(o_ref.dtype)
    elif epilogue == "relu2":
        r = jnp.maximum(acc, 0.0)
        o_ref[...] = (r * r).astype(o_ref.dtype)
    elif epilogue == "glu":
        g = e_ref[...].astype(F32)
        o_ref[...] = (g * (1.0 / (1.0 + jnp.exp(-acc)))).astype(o_ref.dtype)
    elif epilogue == "resid":
        o_ref[...] = (alpha * e_ref[...].astype(F32) + acc).astype(o_ref.dtype)
    else:
        raise ValueError(epilogue)


def _mm_w32_kernel(*refs, n_a, epilogue, alpha, has_extra):
    a_refs = refs[:n_a]
    b_ref = refs[n_a]
    e_ref = refs[n_a + 1] if has_extra else None
    o_ref, wb_ref = refs[-2], refs[-1]

    @pl.when(pl.program_id(1) == 0)
    def _():
        wb_ref[...] = b_ref[...].astype(BF16)

    acc = None
    k0 = 0
    for a_ref in a_refs:
        kw = a_ref.shape[1]
        part = jnp.dot(a_ref[...], wb_ref[k0:k0 + kw, :], preferred_element_type=F32)
        acc = part if acc is None else acc + part
        k0 += kw
    _epilogue(acc, e_ref, o_ref, epilogue, alpha)


def _mm_w32k_kernel(*refs, nk, epilogue, alpha, has_extra):
    a_ref, b_ref = refs[:2]
    e_ref = refs[2] if has_extra else None
    o_ref, wb_ref, acc_ref = refs[-3:]
    i, k = pl.program_id(1), pl.program_id(2)

    @pl.when(i == 0)
    def _():
        wb_ref[k] = b_ref[...].astype(BF16)

    @pl.when(k == 0)
    def _():
        acc_ref[...] = jnp.zeros_like(acc_ref)

    acc_ref[...] += jnp.dot(a_ref[...], wb_ref[k], preferred_element_type=F32)

    @pl.when(k == nk - 1)
    def _():
        _epilogue(acc_ref[...], e_ref, o_ref, epilogue, alpha)


def _matmul_w32k(a, b, layer, *, tm, tn, tk, epilogue, extra, alpha, out_dtype, name):
    m, kdim = a.shape
    _, k2, n = b.shape
    assert kdim == k2
    tm, tn, tk = min(tm, m), min(tn, n), min(tk, kdim)
    assert m % tm == 0 and n % tn == 0 and kdim % tk == 0
    nk = kdim // tk
    has_extra = extra is not None
    in_specs = [pl.BlockSpec((tm, tk), lambda j, i, k: (i, k)),
                pl.BlockSpec((None, tk, tn), lambda j, i, k: (layer, jnp.where(i == 0, k, nk - 1), j))]
    args = [a, b]
    est = 2 * tm * tk * 2 + 2 * tk * tn * 4 + kdim * tn * 2 + tm * tn * 4
    est += 2 * tm * tn * jnp.dtype(out_dtype).itemsize + tm * tn * 4
    if has_extra:
        in_specs.append(pl.BlockSpec((tm, tn), lambda j, i, k: (i, j)))
        est += 2 * tm * tn * extra.dtype.itemsize
        args.append(extra)
    return pl.pallas_call(
        functools.partial(_mm_w32k_kernel, nk=nk, epilogue=epilogue, alpha=alpha, has_extra=has_extra),
        grid=(n // tn, m // tm, nk),
        in_specs=in_specs,
        out_specs=pl.BlockSpec((tm, tn), lambda j, i, k: (i, j)),
        out_shape=jax.ShapeDtypeStruct((m, n), out_dtype),
        scratch_shapes=[pltpu.VMEM((nk, tk, tn), BF16), pltpu.VMEM((tm, tn), F32)],
        compiler_params=pltpu.CompilerParams(
            dimension_semantics=("parallel", "arbitrary", "arbitrary"),
            vmem_limit_bytes=_vmem_limit(est)),
        name=name,
    )(*args)


def _matmul_w32(a_parts, b, layer, *, tm, tn, col_blocks=None, epilogue="cast", extra=None, alpha=1.0,
                out_dtype=BF16, name):
    a_parts = tuple(a_parts)
    m = a_parts[0].shape[0]
    _, kdim, n_all = b.shape
    assert sum(a.shape[1] for a in a_parts) == kdim
    tm, tn = min(tm, m), min(tn, n_all)
    nj = n_all // tn if col_blocks is None else col_blocks
    n = nj * tn
    assert m % tm == 0 and n <= n_all
    has_extra = extra is not None
    in_specs = [pl.BlockSpec((tm, a.shape[1]), lambda j, i: (i, 0)) for a in a_parts]
    in_specs.append(pl.BlockSpec((None, kdim, tn), lambda j, i: (layer, 0, j)))
    args = list(a_parts) + [b]
    est = 2 * tm * kdim * 2 + 2 * kdim * tn * 4 + kdim * tn * 2
    est += 2 * tm * tn * jnp.dtype(out_dtype).itemsize + 2 * tm * tn * 4
    if has_extra:
        if epilogue == "colscale":
            in_specs.append(pl.BlockSpec((1, tn), lambda j, i: (0, j)))
        else:
            in_specs.append(pl.BlockSpec((tm, tn), lambda j, i: (i, j)))
            est += 2 * tm * tn * extra.dtype.itemsize
        args.append(extra)
    return pl.pallas_call(
        functools.partial(_mm_w32_kernel, n_a=len(a_parts), epilogue=epilogue, alpha=alpha,
                          has_extra=has_extra),
        grid=(nj, m // tm),
        in_specs=in_specs,
        out_specs=pl.BlockSpec((tm, tn), lambda j, i: (i, j)),
        out_shape=jax.ShapeDtypeStruct((m, n), out_dtype),
        scratch_shapes=[pltpu.VMEM((kdim, tn), BF16)],
        compiler_params=pltpu.CompilerParams(
            dimension_semantics=("parallel", "arbitrary"),
            vmem_limit_bytes=_vmem_limit(est)),
        name=name,
    )(*args)


def _query_colscale(n_cols, q_start, q_width):
    col = jnp.arange(n_cols)
    is_q = (col >= q_start) & (col < q_start + q_width)
    return jnp.where(is_q, HEAD_DIM ** -0.5 * LOG2E, 1.0).astype(F32).reshape(1, n_cols)


def _ln_kernel(x_ref, g_ref, b_ref, of_ref, ob_ref):
    x = x_ref[...]
    mu = jnp.mean(x, axis=-1, keepdims=True)
    xc = x - mu
    var = jnp.mean(xc * xc, axis=-1, keepdims=True)
    y = xc * lax.rsqrt(var + LN_EPS) * g_ref[...] + b_ref[...]
    of_ref[...] = y
    ob_ref[...] = y.astype(BF16)


def _layer_norm(x, g, b, *, tr, name):
    m, d = x.shape
    tr = min(tr, m)
    assert m % tr == 0
    row = pl.BlockSpec((tr, d), lambda i: (i, 0))
    vec = pl.BlockSpec((1, d), lambda i: (0, 0))
    return pl.pallas_call(
        _ln_kernel,
        grid=(m // tr,),
        in_specs=[row, vec, vec],
        out_specs=[row, row],
        out_shape=[jax.ShapeDtypeStruct((m, d), F32), jax.ShapeDtypeStruct((m, d), BF16)],
        compiler_params=pltpu.CompilerParams(
            dimension_semantics=("parallel",),
            vmem_limit_bytes=_vmem_limit(2 * tr * d * (4 + 4 + 2) + 4 * tr * d * 4)),
        name=name,
    )(x, g.reshape(1, d).astype(F32), b.reshape(1, d).astype(F32))


def _ssm_kernel(u_ref, bm_ref, cm_ref, ar_ref, ai_ref, d_ref, o_ref, sre, sim, st_ref,
                *, lc, pitch, nq, tiles_per_q, n_vregs):
    c = pl.program_id(1)

    @pl.when(c == 0)
    def _():
        st_ref[...] = jnp.zeros_like(st_ref)

    for q in range(nq):
        uq = u_ref[0, :, q * LANES:(q + 1) * LANES]
        res = jnp.dot(uq, bm_ref[q], preferred_element_type=F32)
        for tl in range(tiles_per_q):
            base = (q * tiles_per_q + tl) * pitch
            sre[base:base + lc, :] = res[:, (2 * tl) * LANES:(2 * tl + 1) * LANES]
            sim[base:base + lc, :] = res[:, (2 * tl + 1) * LANES:(2 * tl + 2) * LANES]

    ar = [ar_ref[j * SUBLANES:(j + 1) * SUBLANES, :] for j in range(n_vregs)]
    ai = [ai_ref[j * SUBLANES:(j + 1) * SUBLANES, :] for j in range(n_vregs)]
    xr0 = tuple(st_ref[0, j * SUBLANES:(j + 1) * SUBLANES, :] for j in range(n_vregs))
    xi0 = tuple(st_ref[1, j * SUBLANES:(j + 1) * SUBLANES, :] for j in range(n_vregs))

    def step(t, carry):
        xr, xi = carry
        nxr, nxi = [], []
        for j in range(n_vregs):
            rows = pl.ds(j * SUBLANES * pitch + t, SUBLANES, stride=pitch)
            nr = ar[j] * xr[j] - ai[j] * xi[j] + sre[rows, :]
            ni = ar[j] * xi[j] + ai[j] * xr[j] + sim[rows, :]
            sre[rows, :] = nr
            sim[rows, :] = ni
            nxr.append(nr)
            nxi.append(ni)
        return tuple(nxr), tuple(nxi)

    xr, xi = lax.fori_loop(0, lc, step, (xr0, xi0))
    for j in range(n_vregs):
        st_ref[0, j * SUBLANES:(j + 1) * SUBLANES, :] = xr[j]
        st_ref[1, j * SUBLANES:(j + 1) * SUBLANES, :] = xi[j]

    for q in range(nq):
        pieces = []
        for tl in range(tiles_per_q):
            base = (q * tiles_per_q + tl) * pitch
            pieces.append(sre[base:base + lc, :].astype(BF16))
            pieces.append(sim[base:base + lc, :].astype(BF16))
        xcat = jnp.concatenate(pieces, axis=1)
        y = jnp.dot(xcat, cm_ref[q], preferred_element_type=F32)
        uq = u_ref[0, :, q * LANES:(q + 1) * LANES].astype(F32)
        y = y + d_ref[:, q * LANES:(q + 1) * LANES] * uq
        o_ref[0, :, q * LANES:(q + 1) * LANES] = _gelu_tanh(y).astype(o_ref.dtype)


def _ssm_discretise(a_re, a_im, log_dt, b_re, b_im, c_re, c_im):
    g, n = a_re.shape
    p = b_re.shape[-1]
    lr = a_re.astype(F32)
    li = a_im.astype(F32)
    dt = jnp.exp(log_dt.astype(F32))[:, None]
    mag = jnp.exp(lr * dt)
    abar_r = mag * jnp.cos(li * dt)
    abar_i = mag * jnp.sin(li * dt)
    den = lr * lr + li * li
    nr = abar_r - 1.0
    ni = abar_i
    zr = (nr * lr + ni * li) / den
    zi = (ni * lr - nr * li) / den
    br = b_re.astype(F32)
    bi = b_im.astype(F32)
    bbar_r = zr[..., None] * br - zi[..., None] * bi
    bbar_i = zr[..., None] * bi + zi[..., None] * br

    gq = LANES // p
    gt = LANES // n
    nq = g // gq
    tq = gq // gt
    eye = jnp.eye(gq, dtype=F32)
    bb = jnp.stack([bbar_r, bbar_i], 0).reshape(2, nq, gq, n, p)
    bf = jnp.einsum("aqhnp,gh->qgpahn", bb, eye)
    bf = bf.reshape(nq, gq, p, 2, tq, gt, n).transpose(0, 1, 2, 4, 3, 5, 6)
    bmat = bf.reshape(nq, gq * p, tq * 2 * gt * n).astype(BF16)
    cc = jnp.stack([c_re.astype(F32), -c_im.astype(F32)], 0).reshape(2, nq, gq, p, n)
    cf = jnp.einsum("aqhpn,gh->qahngp", cc, eye)
    cf = cf.reshape(nq, 2, tq, gt, n, gq, p).transpose(0, 2, 1, 3, 4, 5, 6)
    cmat = cf.reshape(nq, tq * 2 * gt * n, gq * p).astype(BF16)
    n_tiles = g * n // LANES
    return bmat, cmat, abar_r.reshape(n_tiles, LANES), abar_i.reshape(n_tiles, LANES)


def _ssm_scan(proj, bmat, cmat, ar, ai, d, *, w_ssm, lc):
    bsz, s, _ = proj.shape
    lc = min(lc, s)
    assert s % lc == 0 and lc % SUBLANES == 0
    nq, _, qcols = bmat.shape
    tiles_per_q = qcols // (2 * LANES)
    n_tiles = ar.shape[0]
    assert n_tiles == nq * tiles_per_q and n_tiles % SUBLANES == 0 and w_ssm == nq * LANES
    n_vregs = n_tiles // SUBLANES
    pitch = lc + 4
    est = (2 * 2 * lc * w_ssm * 2 + 2 * (bmat.size + cmat.size) * 2
           + 2 * n_tiles * pitch * LANES * 4 + 4 * lc * qcols * 4)
    kern = functools.partial(_ssm_kernel, lc=lc, pitch=pitch, nq=nq, tiles_per_q=tiles_per_q,
                             n_vregs=n_vregs)
    const2 = lambda b, c: (0, 0)
    const3 = lambda b, c: (0, 0, 0)
    return pl.pallas_call(
        kern,
        grid=(bsz, s // lc),
        in_specs=[pl.BlockSpec((1, lc, w_ssm), lambda b, c: (b, c, 0)),
                  pl.BlockSpec(bmat.shape, const3),
                  pl.BlockSpec(cmat.shape, const3),
                  pl.BlockSpec(ar.shape, const2),
                  pl.BlockSpec(ai.shape, const2),
                  pl.BlockSpec((1, w_ssm), const2)],
        out_specs=pl.BlockSpec((1, lc, w_ssm), lambda b, c: (b, c, 0)),
        out_shape=jax.ShapeDtypeStruct((bsz, s, w_ssm), BF16),
        scratch_shapes=[pltpu.VMEM((n_tiles * pitch, LANES), F32),
                        pltpu.VMEM((n_tiles * pitch, LANES), F32),
                        pltpu.VMEM((2, n_tiles, LANES), F32)],
        compiler_params=pltpu.CompilerParams(
            dimension_semantics=("parallel", "arbitrary"),
            vmem_limit_bytes=_vmem_limit(est)),
        name="s5_scan",
    )(proj, bmat, cmat, ar, ai, d.reshape(1, w_ssm).astype(F32))


def _attn_tiles(s, tq, tk):
    tq, tk = min(tq, s), min(tk, s)
    assert s % tq == 0 and tq % (2 * tk) == 0
    return tq, tk


def _key_query_iota(tk, tq):
    key = lax.broadcasted_iota(jnp.int32, (tk, tq), 0)
    query = lax.broadcasted_iota(jnp.int32, (tk, tq), 1)
    return key, query


def _transpose_values(v_ref, vt_ref, hs, chunk=512):
    s = v_ref.shape[1]
    chunk = min(chunk, s)

    @pl.when(pl.program_id(2) == 0)
    def _():
        for h in range(hs):
            for c in range(s // chunk):
                rows = slice(c * chunk, (c + 1) * chunk)
                vt_ref[h, :, rows] = v_ref[0, rows, h * HEAD_DIM:(h + 1) * HEAD_DIM].T


def _sb_kernel(q_ref, k_ref, v_ref, o_ref, z_ref, hi_ref, lo_ref, acc_ref, vt_ref, *, tq, tk, hs):
    qi = pl.program_id(2)
    r = tq // tk
    heads = range(hs)
    key, query = _key_query_iota(tk, tq)
    kk = lax.broadcasted_iota(jnp.int32, (tk, tk), 0)
    kk2 = lax.broadcasted_iota(jnp.int32, (tk, tk), 1)
    at_or_after = (kk2 >= kk).astype(BF16)

    def head_cols(h):
        return slice(h * HEAD_DIM, (h + 1) * HEAD_DIM)

    q = [q_ref[0, :, head_cols(h)] for h in heads]
    _transpose_values(v_ref, vt_ref, hs)

    def keep_logs(h, slot, kb, mask):
        k = k_ref[0, pl.ds(pl.multiple_of(kb * tk, tk), tk), head_cols(h)]
        z = lax.dot_general(k, q[h], _NT, preferred_element_type=F32)
        nk = jnp.maximum(z, jnp.log2(1.0 + jnp.exp2(jnp.minimum(z, EXP2_CLAMP))))
        if mask is not None:
            nk = jnp.where(mask, nk, 0.0)
        hi = nk.astype(BF16)
        z_ref[h, slot] = z
        hi_ref[h, slot] = hi
        lo_ref[h, slot] = (nk - hi.astype(F32)).astype(BF16)

    def accumulate(h, slot, kb, carry, mask):
        vt = vt_ref[h, :, pl.ds(pl.multiple_of(kb * tk, tk), tk)]
        incl = (jnp.dot(at_or_after, hi_ref[h, slot], preferred_element_type=F32)
                + jnp.dot(at_or_after, lo_ref[h, slot], preferred_element_type=F32))
        w = jnp.exp2(z_ref[h, slot] - incl - carry)
        if mask is not None:
            w = jnp.where(mask, w, 0.0)
        acc_ref[h] += jnp.dot(vt, w.astype(BF16), preferred_element_type=F32)
        return carry + incl[0:1, :]

    carry = [jnp.zeros((1, tq), F32) for _ in heads]
    acc_ref[...] = jnp.zeros_like(acc_ref)
    for d in reversed(range(r)):
        mask = (key + d * tk) < query
        for h in heads:
            keep_logs(h, d % 2, qi * r + d, mask)
        carry = [accumulate(h, d % 2, qi * r + d, carry[h], mask) for h in heads]

    n_full = qi * r
    for h in heads:
        keep_logs(h, 0, jnp.maximum(n_full - 1, 0), None)

    def body(it, carry):
        kb = n_full - 1 - 2 * it
        for h in heads:
            keep_logs(h, 1, kb - 1, None)
        carry = [accumulate(h, 0, kb, carry[h], None) for h in heads]
        for h in heads:
            keep_logs(h, 0, jnp.maximum(kb - 2, 0), None)
        return [accumulate(h, 1, kb - 1, carry[h], None) for h in heads]

    lax.fori_loop(0, n_full // 2, body, carry)
    for h in heads:
        o_ref[0, :, head_cols(h)] = acc_ref[h].T.astype(o_ref.dtype)


def _sb_attention(proj, *, heads, col0, tq, tk):
    bsz, s, _ = proj.shape
    tq, tk = _attn_tiles(s, tq, tk)
    hs = min(SB_HEADS_PER_STEP, heads)
    wcols = hs * HEAD_DIM
    assert heads % hs == 0 and col0 % wcols == 0 and (heads * HEAD_DIM) % wcols == 0
    c0, hg = col0 // wcols, heads // hs
    est = 2 * (2 * s * wcols * 2) + 4 * tq * wcols * 2 + hs * 16 * tq * tk * 4
    return pl.pallas_call(
        functools.partial(_sb_kernel, tq=tq, tk=tk, hs=hs),
        grid=(bsz, hg, s // tq),
        in_specs=[pl.BlockSpec((1, tq, wcols), lambda b, h, i: (b, i, c0 + h)),
                  pl.BlockSpec((1, s, wcols), lambda b, h, i: (b, 0, c0 + hg + h)),
                  pl.BlockSpec((1, s, wcols), lambda b, h, i: (b, 0, c0 + 2 * hg + h))],
        out_specs=pl.BlockSpec((1, tq, wcols), lambda b, h, i: (b, i, h)),
        out_shape=jax.ShapeDtypeStruct((bsz, s, heads * HEAD_DIM), BF16),
        scratch_shapes=[pltpu.VMEM((hs, 2, tk, tq), F32), pltpu.VMEM((hs, 2, tk, tq), BF16),
                        pltpu.VMEM((hs, 2, tk, tq), BF16), pltpu.VMEM((hs, HEAD_DIM, tq), F32),
                        pltpu.VMEM((hs, HEAD_DIM, s), BF16)],
        compiler_params=pltpu.CompilerParams(
            dimension_semantics=("parallel", "parallel", "arbitrary"),
            vmem_limit_bytes=_vmem_limit(est)),
        name="stickbreak_attn",
    )(proj, proj, proj)


def _cum_kernel(f_ref, b_ref, o_ref, kc_ref, carry_ref, *, chunk, heads):
    ts, w = f_ref.shape[1], f_ref.shape[2]

    @pl.when(pl.program_id(1) == 0)
    def _():
        carry_ref[...] = jnp.zeros_like(carry_ref)

    row = lax.broadcasted_iota(jnp.int32, (chunk, chunk), 0)
    col = lax.broadcasted_iota(jnp.int32, (chunk, chunk), 1)
    tri = (col <= row).astype(F32)
    src = lax.broadcasted_iota(jnp.int32, (CUM_PIECES * w, heads * HEAD_DIM), 0)
    dst = lax.broadcasted_iota(jnp.int32, (CUM_PIECES * w, heads * HEAD_DIM), 1)
    place = ((src % w) * HEAD_DIM + src // w == dst).astype(BF16)
    carry = carry_ref[...]
    for c in range(ts // chunk):
        rows = slice(c * chunk, (c + 1) * chunk)
        ls = _log_sigmoid(f_ref[0, rows, :] + b_ref[...])
        cs = jnp.dot(tri, ls, preferred_element_type=F32, precision=lax.Precision.HIGHEST) + carry
        carry = cs[chunk - 1:chunk, :]
        o_ref[0, rows, :] = cs * LOG2E
    carry_ref[...] = carry
    rest = -o_ref[0]
    pieces = []
    for _ in range(CUM_PIECES):
        piece = rest.astype(BF16)
        pieces.append(piece)
        rest = rest - piece.astype(F32)
    kc_ref[0] = jnp.dot(jnp.concatenate(pieces, axis=1), place, preferred_element_type=F32).astype(BF16)


def _forget_cumsum(f_logit, b_f, *, heads, chunk=256, ts=512):
    bsz, s, w = f_logit.shape
    ts = min(ts, s)
    chunk = min(chunk, ts)
    assert s % ts == 0 and ts % chunk == 0 and heads <= w
    return pl.pallas_call(
        functools.partial(_cum_kernel, chunk=chunk, heads=heads),
        grid=(bsz, s // ts),
        in_specs=[pl.BlockSpec((1, ts, w), lambda b, t: (b, t, 0)), pl.BlockSpec((1, w), lambda b, t: (0, 0))],
        out_specs=[pl.BlockSpec((1, ts, w), lambda b, t: (b, t, 0)),
                   pl.BlockSpec((1, ts, heads * HEAD_DIM), lambda b, t: (b, t, 0))],
        out_shape=[jax.ShapeDtypeStruct((bsz, s, w), F32),
                   jax.ShapeDtypeStruct((bsz, s, heads * HEAD_DIM), BF16)],
        scratch_shapes=[pltpu.VMEM((1, w), F32)],
        compiler_params=pltpu.CompilerParams(dimension_semantics=("parallel", "arbitrary")),
        name="forget_cumsum",
    )(f_logit, b_f)


def _fox_kernel(q_ref, k_ref, kc_ref, v_ref, ci_ref, o_ref, z_ref, p_ref, acc_ref, vt_ref, *, tq, tk, hs):
    qi = pl.program_id(2)
    r = tq // tk
    heads = range(hs)
    lane = lax.broadcasted_iota(jnp.int32, (tq, HEAD_DIM), 1)
    ones = (lane < CUM_PIECES).astype(BF16)
    key, query = _key_query_iota(tk, tq)

    def head_cols(h):
        return slice(h * HEAD_DIM, (h + 1) * HEAD_DIM)

    q = [jnp.concatenate([q_ref[0, :, head_cols(h)], ones], axis=1) for h in heads]
    _transpose_values(v_ref, vt_ref, hs)
    ci = [ci_ref[0, h, pl.ds(qi, 1), :] for h in heads]

    def logits(h, kb):
        rows = pl.ds(pl.multiple_of(kb * tk, tk), tk)
        k = jnp.concatenate([k_ref[0, rows, head_cols(h)], kc_ref[0, rows, head_cols(h)]], axis=1)
        return lax.dot_general(k, q[h], _NT, preferred_element_type=F32)

    def softmax(h, z, st):
        _, m, l = st
        m_new = jnp.maximum(m, jnp.max(z, axis=0, keepdims=True) + ci[h])
        p = jnp.exp2(z - (m_new - ci[h]))
        a = jnp.exp2(m - m_new)
        return p.astype(BF16), (a, m_new, a * l + jnp.sum(p, axis=0, keepdims=True))

    def rescale_add(h, st, p, kb):
        vt = vt_ref[h, :, pl.ds(pl.multiple_of(kb * tk, tk), tk)]
        return st[0] * acc_ref[h] + jnp.dot(vt, p, preferred_element_type=F32)

    st = [(None, jnp.full((1, tq), -jnp.inf, F32), jnp.zeros((1, tq), F32)) for _ in heads]
    acc_ref[...] = jnp.zeros_like(acc_ref)
    for d in range(r):
        mask = (key + d * tk) <= query
        z = [jnp.where(mask, logits(h, qi * r + d), -jnp.inf) for h in heads]
        if d > 0:
            for h in heads:
                acc_ref[h] = rescale_add(h, st[h], p_ref[h, (d - 1) % 2], qi * r + d - 1)
        for h in heads:
            p_ref[h, d % 2], st[h] = softmax(h, z[h], st[h])

    n_full = qi * r
    for h in heads:
        z_ref[h, 0] = logits(h, jnp.maximum(n_full - 1, 0))

    def step(slot, kb, kb_p, st):
        st = list(st)
        for h in heads:
            z_ref[h, 1 - slot] = logits(h, jnp.maximum(kb - 1, 0))
        for h in heads:
            acc_ref[h] = rescale_add(h, st[h], p_ref[h, 1 - slot], kb_p)
        for h in heads:
            p_ref[h, slot], st[h] = softmax(h, z_ref[h, slot], st[h])
        return st

    def body(it, st):
        kb = n_full - 1 - 2 * it
        st = step(0, kb, jnp.where(it == 0, qi * r + r - 1, kb + 1), st)
        return step(1, kb - 1, kb, st)

    st = lax.fori_loop(0, n_full // 2, body, st)
    for h in heads:
        acc = rescale_add(h, st[h], p_ref[h, 1], jnp.where(qi == 0, r - 1, 0))
        o_ref[0, :, head_cols(h)] = (acc / st[h][2]).T.astype(o_ref.dtype)


def _fox_attention(qkv, kc, cum_t, *, heads, tq, tk):
    bsz, s, _ = qkv.shape
    tq, tk = _attn_tiles(s, tq, tk)
    nq = s // tq
    ci = cum_t.reshape(bsz, heads, nq, tq)
    hs = min(FOX_HEADS_PER_STEP, heads)
    wcols = hs * HEAD_DIM
    assert heads % hs == 0
    hg = heads // hs
    est = 2 * (3 * s * wcols * 2) + 4 * tq * wcols * 2 + hs * 16 * tq * tk * 4
    return pl.pallas_call(
        functools.partial(_fox_kernel, tq=tq, tk=tk, hs=hs),
        grid=(bsz, hg, nq),
        in_specs=[pl.BlockSpec((1, tq, wcols), lambda b, h, i: (b, i, h)),
                  pl.BlockSpec((1, s, wcols), lambda b, h, i: (b, 0, hg + h)),
                  pl.BlockSpec((1, s, wcols), lambda b, h, i: (b, 0, h)),
                  pl.BlockSpec((1, s, wcols), lambda b, h, i: (b, 0, 2 * hg + h)),
                  pl.BlockSpec((1, hs, nq, tq), lambda b, h, i: (b, h, 0, 0))],
        out_specs=pl.BlockSpec((1, tq, wcols), lambda b, h, i: (b, i, h)),
        out_shape=jax.ShapeDtypeStruct((bsz, s, heads * HEAD_DIM), BF16),
        scratch_shapes=[pltpu.VMEM((hs, 2, tk, tq), F32), pltpu.VMEM((hs, 2, tk, tq), BF16),
                        pltpu.VMEM((hs, HEAD_DIM, tq), F32), pltpu.VMEM((hs, HEAD_DIM, s), BF16)],
        compiler_params=pltpu.CompilerParams(
            dimension_semantics=("parallel", "parallel", "arbitrary"),
            vmem_limit_bytes=_vmem_limit(est)),
        name="fox_attn",
    )(qkv, qkv, kc, qkv, ci)


ATTN_TQ = 512
ATTN_TK = 256
SB_HEADS_PER_STEP = 2
FOX_HEADS_PER_STEP = 4


def _mlp_and_norms(h32, hb, mix_fn, ln_mix_g, ln_mix_b, w1, w2, ln_ffn_g, ln_ffn_b, *, layer, alpha):
    z = mix_fn(h32, hb)
    h32, hb = _layer_norm(z, ln_mix_g[layer], ln_mix_b[layer], tr=256, name=f"ln_mix_{layer}")
    hid = _matmul_w32([hb], w1, layer, tm=1024, tn=512, epilogue="relu2", name=f"mlp_up_{layer}")
    z = _matmul_w32k(hid, w2, layer, tm=1024, tn=512, tk=2048, epilogue="resid", extra=h32,
                     alpha=alpha, out_dtype=F32, name=f"mlp_down_{layer}")
    return _layer_norm(z, ln_ffn_g[layer], ln_ffn_b[layer], tr=256, name=f"ln_ffn_{layer}")


def kernel(x, even_w_in, ssm_a_re, ssm_a_im, ssm_log_dt, ssm_b_re, ssm_b_im, ssm_c_re, ssm_c_im, ssm_d,
           ssm_w_glu, even_w_out, fox_w_in, fox_b_f, fox_w_out, ln_mix_g, ln_mix_b, mlp_w1, mlp_w2,
           ln_ffn_g, ln_ffn_b):
    bsz, s, d_model = x.shape
    depth = ln_mix_g.shape[0]
    alpha = (2.0 * depth) ** 0.25
    m = bsz * s
    w_ssm = ssm_d.shape[-1]

    h32 = x.reshape(m, d_model)
    hb = h32.astype(BF16)
    for layer in range(depth):
        i = layer // 2
        if layer % 2 == 0:
            def mix_fn(h32, hb, i=i):
                n_in = even_w_in.shape[-1]
                sb_heads = (n_in - w_ssm) // (3 * HEAD_DIM)
                proj = _matmul_w32([hb], even_w_in, i, tm=1024, tn=512, epilogue="colscale",
                                   extra=_query_colscale(n_in, w_ssm, sb_heads * HEAD_DIM), name="even_in")
                proj3 = proj.reshape(bsz, s, -1)
                bmat, cmat, ar, ai = _ssm_discretise(ssm_a_re[i], ssm_a_im[i], ssm_log_dt[i], ssm_b_re[i],
                                                     ssm_b_im[i], ssm_c_re[i], ssm_c_im[i])
                g = _ssm_scan(proj3, bmat, cmat, ar, ai, ssm_d[i], w_ssm=w_ssm, lc=256).reshape(m, w_ssm)
                y_ssm = _matmul_w32([g], ssm_w_glu, i, tm=1024, tn=512, epilogue="glu", extra=g, name="ssm_glu")
                y_sb = _sb_attention(proj3, heads=sb_heads, col0=w_ssm, tq=ATTN_TQ, tk=ATTN_TK).reshape(m, -1)
                return _matmul_w32([y_ssm, y_sb], even_w_out, i, tm=1024, tn=512, epilogue="resid",
                                   extra=h32, alpha=alpha, out_dtype=F32, name="even_out")
        else:
            def mix_fn(h32, hb, i=i):
                heads = fox_b_f.shape[-1]
                width = heads * HEAD_DIM
                qkv = _matmul_w32([hb], fox_w_in, i, tm=1024, tn=512, col_blocks=3 * width // 512,
                                  epilogue="colscale", extra=_query_colscale(3 * width, 0, width), name="fox_in")
                qkv = qkv.reshape(bsz, s, 3 * width)
                w_f = jnp.pad(fox_w_in[i:i + 1, :, 3 * width:], ((0, 0), (0, 0), (0, LANES - heads)))
                b_f = jnp.pad(fox_b_f[i].astype(F32), (0, LANES - heads)).reshape(1, LANES)
                f_logit = _matmul_w32([hb], w_f, 0, tm=1024, tn=LANES, out_dtype=F32, name="fox_forget")
                cum, kc = _forget_cumsum(f_logit.reshape(bsz, s, LANES), b_f, heads=heads)
                cum_t = cum[:, :, :heads].transpose(0, 2, 1)
                y = _fox_attention(qkv, kc, cum_t, heads=heads, tq=ATTN_TQ, tk=ATTN_TK)
                return _matmul_w32([y.reshape(m, width)], fox_w_out, i, tm=1024, tn=512, epilogue="resid",
                                   extra=h32, alpha=alpha, out_dtype=F32, name="fox_out")
        h32, hb = _mlp_and_norms(h32, hb, mix_fn, ln_mix_g, ln_mix_b, mlp_w1, mlp_w2, ln_ffn_g, ln_ffn_b,
                                 layer=layer, alpha=alpha)
    return h32.reshape(bsz, s, d_model)
```

```python
import functools
import math

import jax
import jax.numpy as jnp
from jax import lax
from jax.experimental import pallas as pl
from jax.experimental.pallas import tpu as pltpu

F32 = jnp.float32
BF16 = jnp.bfloat16

HEAD_DIM = 128
LN_EPS = 1e-5
LOG2E = math.log2(math.e)
LANES = 128
SUBLANES = 8
MXU_DIM = 256
VMEM_CAP = 56 * 1024 * 1024
CUM_PIECES = 3
EXP2_CLAMP = 126.0

_NT = (((1,), (1,)), ((), ()))
_TN = (((0,), (0,)), ((), ()))


def _vmem_limit(estimate_bytes):
    return int(min(VMEM_CAP, max(32 * 1024 * 1024, estimate_bytes * 5 // 4)))


def _log_sigmoid(x):
    return jnp.minimum(x, 0.0) - jnp.log(1.0 + jnp.exp(-jnp.abs(x)))


def _gelu_tanh(x):
    c = math.sqrt(2.0 / math.pi)
    return 0.5 * x * (1.0 + jnp.tanh(c * (x + 0.044715 * (x * x * x))))


def _epilogue(acc, e_ref, o_ref, epilogue, alpha):
    if epilogue == "cast":
        o_ref[...] = acc.astype(o_ref.dtype)
    elif epilogue == "colscale":
        o_ref[...] = (acc * e_ref[...]).astype(o_ref.dtype)
    elif epilogue == "relu2":
        r = jnp.maximum(acc, 0.0)
        o_ref[...] = (r * r).astype(o_ref.dtype)
    elif epilogue == "glu":
        g = e_ref[...].astype(F32)
        o_ref[...] = (g * (1.0 / (1.0 + jnp.exp(-acc)))).astype(o_ref.dtype)
    elif epilogue == "resid":
        o_ref[...] = (alpha * e_ref[...].astype(F32) + acc).astype(o_ref.dtype)
    else:
        raise ValueError(epilogue)


def _mm_w32_kernel(*refs, n_a, epilogue, alpha, has_extra, b_is_nk):
    a_refs = refs[:n_a]
    b_ref = refs[n_a]
    e_ref = refs[n_a + 1] if has_extra else None
    o_ref, wb_ref = refs[-2], refs[-1]

    @pl.when(pl.program_id(1) == 0)
    def _():
        w = b_ref[...].T if b_is_nk else b_ref[...]
        wb_ref[...] = w.astype(BF16)

    acc = None
    k0 = 0
    for a_ref in a_refs:
        kw = a_ref.shape[1]
        part = jnp.dot(a_ref[...], wb_ref[k0:k0 + kw, :], preferred_element_type=F32)
        acc = part if acc is None else acc + part
        k0 += kw
    _epilogue(acc, e_ref, o_ref, epilogue, alpha)


def _mm_ktiled_kernel(a_ref, b_ref, e_ref, o_ref, acc_ref, *, nk, epilogue, alpha):
    k = pl.program_id(2)

    @pl.when(k == 0)
    def _():
        acc_ref[...] = jnp.zeros_like(acc_ref)

    acc_ref[...] += jnp.dot(a_ref[...], b_ref[...], preferred_element_type=F32)

    @pl.when(k == nk - 1)
    def _():
        _epilogue(acc_ref[...], e_ref, o_ref, epilogue, alpha)


def _matmul_ktiled(a, b, layer, *, tm, tn, tk, epilogue, extra, alpha, out_dtype, name):
    m, kdim = a.shape
    _, k2, n = b.shape
    assert kdim == k2
    tm, tn, tk = min(tm, m), min(tn, n), min(tk, kdim)
    assert m % tm == 0 and n % tn == 0 and kdim % tk == 0
    nk = kdim // tk
    est = 2 * (tm * tk + tk * tn) * 2 + 2 * tm * tn * (jnp.dtype(out_dtype).itemsize + extra.dtype.itemsize)
    est += 2 * tm * tn * 4
    return pl.pallas_call(
        functools.partial(_mm_ktiled_kernel, nk=nk, epilogue=epilogue, alpha=alpha),
        grid=(m // tm, n // tn, nk),
        in_specs=[pl.BlockSpec((tm, tk), lambda i, j, k: (i, k)),
                  pl.BlockSpec((None, tk, tn), lambda i, j, k: (layer, k, j)),
                  pl.BlockSpec((tm, tn), lambda i, j, k: (i, j))],
        out_specs=pl.BlockSpec((tm, tn), lambda i, j, k: (i, j)),
        out_shape=jax.ShapeDtypeStruct((m, n), out_dtype),
        scratch_shapes=[pltpu.VMEM((tm, tn), F32)],
        compiler_params=pltpu.CompilerParams(
            dimension_semantics=("parallel", "parallel", "arbitrary"),
            vmem_limit_bytes=_vmem_limit(est)),
        name=name,
    )(a, b, extra)


def _matmul_w32(a_parts, b, layer, *, tm, tn, col_blocks=None, b_is_nk=False, epilogue="cast", extra=None,
                alpha=1.0, out_dtype=BF16, name):
    a_parts = tuple(a_parts)
    m = a_parts[0].shape[0]
    kdim, n_all = (b.shape[2], b.shape[1]) if b_is_nk else (b.shape[1], b.shape[2])
    assert sum(a.shape[1] for a in a_parts) == kdim
    tm, tn = min(tm, m), min(tn, n_all)
    nj = n_all // tn if col_blocks is None else col_blocks
    n = nj * tn
    assert m % tm == 0 and n <= n_all
    has_extra = extra is not None
    in_specs = [pl.BlockSpec((tm, a.shape[1]), lambda j, i: (i, 0)) for a in a_parts]
    if b_is_nk:
        in_specs.append(pl.BlockSpec((None, tn, kdim), lambda j, i: (layer, j, 0)))
    else:
        in_specs.append(pl.BlockSpec((None, kdim, tn), lambda j, i: (layer, 0, j)))
    args = list(a_parts) + [b]
    est = 2 * tm * kdim * 2 + 2 * kdim * tn * 4 + kdim * tn * 2
    est += 2 * tm * tn * jnp.dtype(out_dtype).itemsize + 2 * tm * tn * 4
    if has_extra:
        if epilogue == "colscale":
            in_specs.append(pl.BlockSpec((1, tn), lambda j, i: (0, j)))
        else:
            in_specs.append(pl.BlockSpec((tm, tn), lambda j, i: (i, j)))
            est += 2 * tm * tn * extra.dtype.itemsize
        args.append(extra)
    return pl.pallas_call(
        functools.partial(_mm_w32_kernel, n_a=len(a_parts), epilogue=epilogue, alpha=alpha,
                          has_extra=has_extra, b_is_nk=b_is_nk),
        grid=(nj, m // tm),
        in_specs=in_specs,
        out_specs=pl.BlockSpec((tm, tn), lambda j, i: (i, j)),
        out_shape=jax.ShapeDtypeStruct((m, n), out_dtype),
        scratch_shapes=[pltpu.VMEM((kdim, tn), BF16)],
        compiler_params=pltpu.CompilerParams(
            dimension_semantics=("parallel", "arbitrary"),
            vmem_limit_bytes=_vmem_limit(est)),
        name=name,
    )(*args)


def _query_colscale(n_cols, q_start, q_width):
    col = jnp.arange(n_cols)
    is_q = (col >= q_start) & (col < q_start + q_width)
    return jnp.where(is_q, HEAD_DIM ** -0.5 * LOG2E, 1.0).astype(F32).reshape(1, n_cols)


def _ln_kernel(x_ref, g_ref, b_ref, of_ref, ob_ref):
    x = x_ref[...]
    mu = jnp.mean(x, axis=-1, keepdims=True)
    xc = x - mu
    var = jnp.mean(xc * xc, axis=-1, keepdims=True)
    y = xc * lax.rsqrt(var + LN_EPS) * g_ref[...] + b_ref[...]
    of_ref[...] = y
    ob_ref[...] = y.astype(BF16)


def _layer_norm(x, g, b, *, tr, name):
    m, d = x.shape
    tr = min(tr, m)
    assert m % tr == 0
    row = pl.BlockSpec((tr, d), lambda i: (i, 0))
    vec = pl.BlockSpec((1, d), lambda i: (0, 0))
    return pl.pallas_call(
        _ln_kernel,
        grid=(m // tr,),
        in_specs=[row, vec, vec],
        out_specs=[row, row],
        out_shape=[jax.ShapeDtypeStruct((m, d), F32), jax.ShapeDtypeStruct((m, d), BF16)],
        compiler_params=pltpu.CompilerParams(
            dimension_semantics=("parallel",),
            vmem_limit_bytes=_vmem_limit(2 * tr * d * (4 + 4 + 2) + 4 * tr * d * 4)),
        name=name,
    )(x, g.reshape(1, d).astype(F32), b.reshape(1, d).astype(F32))


def _ssm_kernel(u_ref, bm_ref, cm_ref, ar_ref, ai_ref, d_ref, o_ref, sre, sim, st_ref,
                *, lc, pitch, nq, tiles_per_q, n_vregs):
    c = pl.program_id(1)

    @pl.when(c == 0)
    def _():
        st_ref[...] = jnp.zeros_like(st_ref)

    for q in range(nq):
        uq = u_ref[0, :, q * LANES:(q + 1) * LANES]
        res = jnp.dot(uq, bm_ref[q], preferred_element_type=F32)
        for tl in range(tiles_per_q):
            base = (q * tiles_per_q + tl) * pitch
            sre[base:base + lc, :] = res[:, (2 * tl) * LANES:(2 * tl + 1) * LANES]
            sim[base:base + lc, :] = res[:, (2 * tl + 1) * LANES:(2 * tl + 2) * LANES]

    ar = [ar_ref[j * SUBLANES:(j + 1) * SUBLANES, :] for j in range(n_vregs)]
    ai = [ai_ref[j * SUBLANES:(j + 1) * SUBLANES, :] for j in range(n_vregs)]
    xr0 = tuple(st_ref[0, j * SUBLANES:(j + 1) * SUBLANES, :] for j in range(n_vregs))
    xi0 = tuple(st_ref[1, j * SUBLANES:(j + 1) * SUBLANES, :] for j in range(n_vregs))

    def step(t, carry):
        xr, xi = carry
        nxr, nxi = [], []
        for j in range(n_vregs):
            rows = pl.ds(j * SUBLANES * pitch + t, SUBLANES, stride=pitch)
            nr = ar[j] * xr[j] - ai[j] * xi[j] + sre[rows, :]
            ni = ar[j] * xi[j] + ai[j] * xr[j] + sim[rows, :]
            sre[rows, :] = nr
            sim[rows, :] = ni
            nxr.append(nr)
            nxi.append(ni)
        return tuple(nxr), tuple(nxi)

    xr, xi = lax.fori_loop(0, lc, step, (xr0, xi0))
    for j in range(n_vregs):
        st_ref[0, j * SUBLANES:(j + 1) * SUBLANES, :] = xr[j]
        st_ref[1, j * SUBLANES:(j + 1) * SUBLANES, :] = xi[j]

    for q in range(nq):
        pieces = []
        for tl in range(tiles_per_q):
            base = (q * tiles_per_q + tl) * pitch
            pieces.append(sre[base:base + lc, :].astype(BF16))
            pieces.append(sim[base:base + lc, :].astype(BF16))
        xcat = jnp.concatenate(pieces, axis=1)
        y = jnp.dot(xcat, cm_ref[q], preferred_element_type=F32)
        uq = u_ref[0, :, q * LANES:(q + 1) * LANES].astype(F32)
        y = y + d_ref[:, q * LANES:(q + 1) * LANES] * uq
        o_ref[0, :, q * LANES:(q + 1) * LANES] = _gelu_tanh(y).astype(o_ref.dtype)


def _ssm_discretise(a_re, a_im, log_dt, b_re, b_im, c_re, c_im):
    g, n = a_re.shape
    p = b_re.shape[-1]
    lr = a_re.astype(F32)
    li = a_im.astype(F32)
    dt = jnp.exp(log_dt.astype(F32))[:, None]
    mag = jnp.exp(lr * dt)
    abar_r = mag * jnp.cos(li * dt)
    abar_i = mag * jnp.sin(li * dt)
    den = lr * lr + li * li
    nr = abar_r - 1.0
    ni = abar_i
    zr = (nr * lr + ni * li) / den
    zi = (ni * lr - nr * li) / den
    br = b_re.astype(F32)
    bi = b_im.astype(F32)
    bbar_r = zr[..., None] * br - zi[..., None] * bi
    bbar_i = zr[..., None] * bi + zi[..., None] * br

    gq = LANES // p
    gt = LANES // n
    nq = g // gq
    tq = gq // gt
    eye = jnp.eye(gq, dtype=F32)
    bb = jnp.stack([bbar_r, bbar_i], 0).reshape(2, nq, gq, n, p)
    bf = jnp.einsum("aqhnp,gh->qgpahn", bb, eye)
    bf = bf.reshape(nq, gq, p, 2, tq, gt, n).transpose(0, 1, 2, 4, 3, 5, 6)
    bmat = bf.reshape(nq, gq * p, tq * 2 * gt * n).astype(BF16)
    cc = jnp.stack([c_re.astype(F32), -c_im.astype(F32)], 0).reshape(2, nq, gq, p, n)
    cf = jnp.einsum("aqhpn,gh->qahngp", cc, eye)
    cf = cf.reshape(nq, 2, tq, gt, n, gq, p).transpose(0, 2, 1, 3, 4, 5, 6)
    cmat = cf.reshape(nq, tq * 2 * gt * n, gq * p).astype(BF16)
    n_tiles = g * n // LANES
    return bmat, cmat, abar_r.reshape(n_tiles, LANES), abar_i.reshape(n_tiles, LANES)


def _ssm_scan(proj, bmat, cmat, ar, ai, d, *, w_ssm, lc):
    bsz, s, _ = proj.shape
    lc = min(lc, s)
    assert s % lc == 0 and lc % SUBLANES == 0
    nq, _, qcols = bmat.shape
    tiles_per_q = qcols // (2 * LANES)
    n_tiles = ar.shape[0]
    assert n_tiles == nq * tiles_per_q and n_tiles % SUBLANES == 0 and w_ssm == nq * LANES
    n_vregs = n_tiles // SUBLANES
    pitch = lc + 4
    est = (2 * 2 * lc * w_ssm * 2 + 2 * (bmat.size + cmat.size) * 2
           + 2 * n_tiles * pitch * LANES * 4 + 4 * lc * qcols * 4)
    kern = functools.partial(_ssm_kernel, lc=lc, pitch=pitch, nq=nq, tiles_per_q=tiles_per_q,
                             n_vregs=n_vregs)
    const2 = lambda b, c: (0, 0)
    const3 = lambda b, c: (0, 0, 0)
    return pl.pallas_call(
        kern,
        grid=(bsz, s // lc),
        in_specs=[pl.BlockSpec((1, lc, w_ssm), lambda b, c: (b, c, 0)),
                  pl.BlockSpec(bmat.shape, const3),
                  pl.BlockSpec(cmat.shape, const3),
                  pl.BlockSpec(ar.shape, const2),
                  pl.BlockSpec(ai.shape, const2),
                  pl.BlockSpec((1, w_ssm), const2)],
        out_specs=pl.BlockSpec((1, lc, w_ssm), lambda b, c: (b, c, 0)),
        out_shape=jax.ShapeDtypeStruct((bsz, s, w_ssm), BF16),
        scratch_shapes=[pltpu.VMEM((n_tiles * pitch, LANES), F32),
                        pltpu.VMEM((n_tiles * pitch, LANES), F32),
                        pltpu.VMEM((2, n_tiles, LANES), F32)],
        compiler_params=pltpu.CompilerParams(
            dimension_semantics=("parallel", "arbitrary"),
            vmem_limit_bytes=_vmem_limit(est)),
        name="s5_scan",
    )(proj, bmat, cmat, ar, ai, d.reshape(1, w_ssm).astype(F32))


def _attn_tiles(s, tq, tk):
    tq, tk = min(tq, s), min(tk, s)
    assert s % tq == 0 and tq % (2 * tk) == 0
    return tq, tk


def _key_query_iota(tk, tq):
    key = lax.broadcasted_iota(jnp.int32, (tk, tq), 0)
    query = lax.broadcasted_iota(jnp.int32, (tk, tq), 1)
    return key, query


def _transpose_values(v_ref, vt_ref, hs, chunk=512):
    s = v_ref.shape[1]
    chunk = min(chunk, s)

    @pl.when(pl.program_id(2) == 0)
    def _():
        for h in range(hs):
            for c in range(s // chunk):
                rows = slice(c * chunk, (c + 1) * chunk)
                vt_ref[h, :, rows] = v_ref[0, rows, h * HEAD_DIM:(h + 1) * HEAD_DIM].T


def _sb_kernel(q_ref, k_ref, v_ref, o_ref, z_ref, hi_ref, lo_ref, acc_ref, vt_ref, *, tq, tk, hs):
    qi = pl.program_id(2)
    r = tq // tk
    heads = range(hs)
    key, query = _key_query_iota(tk, tq)
    kk = lax.broadcasted_iota(jnp.int32, (tk, tk), 0)
    kk2 = lax.broadcasted_iota(jnp.int32, (tk, tk), 1)
    at_or_after = (kk2 >= kk).astype(BF16)

    def head_cols(h):
        return slice(h * HEAD_DIM, (h + 1) * HEAD_DIM)

    q = [q_ref[0, :, head_cols(h)] for h in heads]
    _transpose_values(v_ref, vt_ref, hs)

    def keep_logs(h, slot, kb, mask):
        k = k_ref[0, pl.ds(pl.multiple_of(kb * tk, tk), tk), head_cols(h)]
        z = lax.dot_general(k, q[h], _NT, preferred_element_type=F32)
        nk = jnp.maximum(z, jnp.log2(1.0 + jnp.exp2(jnp.minimum(z, EXP2_CLAMP))))
        if mask is not None:
            nk = jnp.where(mask, nk, 0.0)
        hi = nk.astype(BF16)
        z_ref[h, slot] = z
        hi_ref[h, slot] = hi
        lo_ref[h, slot] = (nk - hi.astype(F32)).astype(BF16)

    def accumulate(h, slot, kb, carry, mask):
        vt = vt_ref[h, :, pl.ds(pl.multiple_of(kb * tk, tk), tk)]
        incl = (jnp.dot(at_or_after, hi_ref[h, slot], preferred_element_type=F32)
                + jnp.dot(at_or_after, lo_ref[h, slot], preferred_element_type=F32))
        w = jnp.exp2(z_ref[h, slot] - incl - carry)
        if mask is not None:
            w = jnp.where(mask, w, 0.0)
        acc_ref[h] += jnp.dot(vt, w.astype(BF16), preferred_element_type=F32)
        return carry + incl[0:1, :]

    carry = [jnp.zeros((1, tq), F32) for _ in heads]
    acc_ref[...] = jnp.zeros_like(acc_ref)
    for d in reversed(range(r)):
        mask = (key + d * tk) < query
        for h in heads:
            keep_logs(h, d % 2, qi * r + d, mask)
        carry = [accumulate(h, d % 2, qi * r + d, carry[h], mask) for h in heads]

    n_full = qi * r
    for h in heads:
        keep_logs(h, 0, jnp.maximum(n_full - 1, 0), None)

    def body(it, carry):
        kb = n_full - 1 - 2 * it
        for h in heads:
            keep_logs(h, 1, kb - 1, None)
        carry = [accumulate(h, 0, kb, carry[h], None) for h in heads]
        for h in heads:
            keep_logs(h, 0, jnp.maximum(kb - 2, 0), None)
        return [accumulate(h, 1, kb - 1, carry[h], None) for h in heads]

    lax.fori_loop(0, n_full // 2, body, carry)
    for h in heads:
        o_ref[0, :, head_cols(h)] = acc_ref[h].T.astype(o_ref.dtype)


def _sb_attention(proj, *, heads, col0, tq, tk):
    bsz, s, _ = proj.shape
    tq, tk = _attn_tiles(s, tq, tk)
    hs = min(SB_HEADS_PER_STEP, heads)
    wcols = hs * HEAD_DIM
    assert heads % hs == 0 and col0 % wcols == 0 and (heads * HEAD_DIM) % wcols == 0
    c0, hg = col0 // wcols, heads // hs
    est = 2 * (2 * s * wcols * 2) + 4 * tq * wcols * 2 + hs * 16 * tq * tk * 4
    return pl.pallas_call(
        functools.partial(_sb_kernel, tq=tq, tk=tk, hs=hs),
        grid=(bsz, hg, s // tq),
        in_specs=[pl.BlockSpec((1, tq, wcols), lambda b, h, i: (b, i, c0 + h)),
                  pl.BlockSpec((1, s, wcols), lambda b, h, i: (b, 0, c0 + hg + h)),
                  pl.BlockSpec((1, s, wcols), lambda b, h, i: (b, 0, c0 + 2 * hg + h))],
        out_specs=pl.BlockSpec((1, tq, wcols), lambda b, h, i: (b, i, h)),
        out_shape=jax.ShapeDtypeStruct((bsz, s, heads * HEAD_DIM), BF16),
        scratch_shapes=[pltpu.VMEM((hs, 2, tk, tq), F32), pltpu.VMEM((hs, 2, tk, tq), BF16),
                        pltpu.VMEM((hs, 2, tk, tq), BF16), pltpu.VMEM((hs, HEAD_DIM, tq), F32),
                        pltpu.VMEM((hs, HEAD_DIM, s), BF16)],
        compiler_params=pltpu.CompilerParams(
            dimension_semantics=("parallel", "parallel", "arbitrary"),
            vmem_limit_bytes=_vmem_limit(est)),
        name="stickbreak_attn",
    )(proj, proj, proj)


def _cum_kernel(f_ref, b_ref, o_ref, kc_ref, carry_ref, *, chunk, heads):
    ts, w = f_ref.shape[1], f_ref.shape[2]

    @pl.when(pl.program_id(1) == 0)
    def _():
        carry_ref[...] = jnp.zeros_like(carry_ref)

    row = lax.broadcasted_iota(jnp.int32, (chunk, chunk), 0)
    col = lax.broadcasted_iota(jnp.int32, (chunk, chunk), 1)
    tri = (col <= row).astype(F32)
    src = lax.broadcasted_iota(jnp.int32, (CUM_PIECES * w, heads * HEAD_DIM), 0)
    dst = lax.broadcasted_iota(jnp.int32, (CUM_PIECES * w, heads * HEAD_DIM), 1)
    place = ((src % w) * HEAD_DIM + src // w == dst).astype(BF16)
    carry = carry_ref[...]
    for c in range(ts // chunk):
        rows = slice(c * chunk, (c + 1) * chunk)
        ls = _log_sigmoid(f_ref[0, rows, :] + b_ref[...])
        cs = jnp.dot(tri, ls, preferred_element_type=F32, precision=lax.Precision.HIGHEST) + carry
        carry = cs[chunk - 1:chunk, :]
        o_ref[0, rows, :] = cs * LOG2E
    carry_ref[...] = carry
    rest = -o_ref[0]
    pieces = []
    for _ in range(CUM_PIECES):
        piece = rest.astype(BF16)
        pieces.append(piece)
        rest = rest - piece.astype(F32)
    kc_ref[0] = jnp.dot(jnp.concatenate(pieces, axis=1), place, preferred_element_type=F32).astype(BF16)


def _forget_cumsum(f_logit, b_f, *, heads, chunk=256, ts=512):
    bsz, s, w = f_logit.shape
    ts = min(ts, s)
    chunk = min(chunk, ts)
    assert s % ts == 0 and ts % chunk == 0 and heads <= w
    return pl.pallas_call(
        functools.partial(_cum_kernel, chunk=chunk, heads=heads),
        grid=(bsz, s // ts),
        in_specs=[pl.BlockSpec((1, ts, w), lambda b, t: (b, t, 0)), pl.BlockSpec((1, w), lambda b, t: (0, 0))],
        out_specs=[pl.BlockSpec((1, ts, w), lambda b, t: (b, t, 0)),
                   pl.BlockSpec((1, ts, heads * HEAD_DIM), lambda b, t: (b, t, 0))],
        out_shape=[jax.ShapeDtypeStruct((bsz, s, w), F32),
                   jax.ShapeDtypeStruct((bsz, s, heads * HEAD_DIM), BF16)],
        scratch_shapes=[pltpu.VMEM((1, w), F32)],
        compiler_params=pltpu.CompilerParams(dimension_semantics=("parallel", "arbitrary")),
        name="forget_cumsum",
    )(f_logit, b_f)


def _fox_kernel(q_ref, k_ref, kc_ref, v_ref, ci_ref, o_ref, z_ref, p_ref, acc_ref, vt_ref, *, tq, tk, hs):
    qi = pl.program_id(2)
    r = tq // tk
    heads = range(hs)
    lane = lax.broadcasted_iota(jnp.int32, (tq, HEAD_DIM), 1)
    ones = (lane < CUM_PIECES).astype(BF16)
    key, query = _key_query_iota(tk, tq)

    def head_cols(h):
        return slice(h * HEAD_DIM, (h + 1) * HEAD_DIM)

    q = [jnp.concatenate([q_ref[0, :, head_cols(h)], ones], axis=1) for h in heads]
    _transpose_values(v_ref, vt_ref, hs)
    ci = [ci_ref[0, h, pl.ds(qi, 1), :] for h in heads]

    def logits(h, kb):
        rows = pl.ds(pl.multiple_of(kb * tk, tk), tk)
        k = jnp.concatenate([k_ref[0, rows, head_cols(h)], kc_ref[0, rows, head_cols(h)]], axis=1)
        return lax.dot_general(k, q[h], _NT, preferred_element_type=F32)

    def softmax(h, z, st):
        _, m, l = st
        m_new = jnp.maximum(m, jnp.max(z, axis=0, keepdims=True) + ci[h])
        p = jnp.exp2(z - (m_new - ci[h]))
        a = jnp.exp2(m - m_new)
        return p.astype(BF16), (a, m_new, a * l + jnp.sum(p, axis=0, keepdims=True))

    def rescale_add(h, st, p, kb):
        vt = vt_ref[h, :, pl.ds(pl.multiple_of(kb * tk, tk), tk)]
        return st[0] * acc_ref[h] + jnp.dot(vt, p, preferred_element_type=F32)

    st = [(None, jnp.full((1, tq), -jnp.inf, F32), jnp.zeros((1, tq), F32)) for _ in heads]
    acc_ref[...] = jnp.zeros_like(acc_ref)
    for d in range(r):
        mask = (key + d * tk) <= query
        z = [jnp.where(mask, logits(h, qi * r + d), -jnp.inf) for h in heads]
        if d > 0:
            for h in heads:
                acc_ref[h] = rescale_add(h, st[h], p_ref[h, (d - 1) % 2], qi * r + d - 1)
        for h in heads:
            p_ref[h, d % 2], st[h] = softmax(h, z[h], st[h])

    n_full = qi * r
    for h in heads:
        z_ref[h, 0] = logits(h, jnp.maximum(n_full - 1, 0))

    def step(slot, kb, kb_p, st):
        st = list(st)
        for h in heads:
            z_ref[h, 1 - slot] = logits(h, jnp.maximum(kb - 1, 0))
        for h in heads:
            acc_ref[h] = rescale_add(h, st[h], p_ref[h, 1 - slot], kb_p)
        for h in heads:
            p_ref[h, slot], st[h] = softmax(h, z_ref[h, slot], st[h])
        return st

    def body(it, st):
        kb = n_full - 1 - 2 * it
        st = step(0, kb, jnp.where(it == 0, qi * r + r - 1, kb + 1), st)
        return step(1, kb - 1, kb, st)

    st = lax.fori_loop(0, n_full // 2, body, st)
    for h in heads:
        acc = rescale_add(h, st[h], p_ref[h, 1], jnp.where(qi == 0, r - 1, 0))
        o_ref[0, :, head_cols(h)] = (acc / st[h][2]).T.astype(o_ref.dtype)


def _fox_attention(qkv, kc, cum_t, *, heads, tq, tk):
    bsz, s, _ = qkv.shape
    tq, tk = _attn_tiles(s, tq, tk)
    nq = s // tq
    ci = cum_t.reshape(bsz, heads, nq, tq)
    hs = min(FOX_HEADS_PER_STEP, heads)
    wcols = hs * HEAD_DIM
    assert heads % hs == 0
    hg = heads // hs
    est = 2 * (3 * s * wcols * 2) + 4 * tq * wcols * 2 + hs * 16 * tq * tk * 4
    return pl.pallas_call(
        functools.partial(_fox_kernel, tq=tq, tk=tk, hs=hs),
        grid=(bsz, hg, nq),
        in_specs=[pl.BlockSpec((1, tq, wcols), lambda b, h, i: (b, i, h)),
                  pl.BlockSpec((1, s, wcols), lambda b, h, i: (b, 0, hg + h)),
                  pl.BlockSpec((1, s, wcols), lambda b, h, i: (b, 0, h)),
                  pl.BlockSpec((1, s, wcols), lambda b, h, i: (b, 0, 2 * hg + h)),
                  pl.BlockSpec((1, hs, nq, tq), lambda b, h, i: (b, h, 0, 0))],
        out_specs=pl.BlockSpec((1, tq, wcols), lambda b, h, i: (b, i, h)),
        out_shape=jax.ShapeDtypeStruct((bsz, s, heads * HEAD_DIM), BF16),
        scratch_shapes=[pltpu.VMEM((hs, 2, tk, tq), F32), pltpu.VMEM((hs, 2, tk, tq), BF16),
                        pltpu.VMEM((hs, HEAD_DIM, tq), F32), pltpu.VMEM((hs, HEAD_DIM, s), BF16)],
        compiler_params=pltpu.CompilerParams(
            dimension_semantics=("parallel", "parallel", "arbitrary"),
            vmem_limit_bytes=_vmem_limit(est)),
        name="fox_attn",
    )(qkv, qkv, kc, qkv, ci)


ATTN_TQ = 512
ATTN_TK = 256
SB_HEADS_PER_STEP = 4
FOX_HEADS_PER_STEP = 4


def _mlp_and_norms(h32, hb, mix_fn, ln_mix_g, ln_mix_b, w1, w2, ln_ffn_g, ln_ffn_b, *, layer, alpha):
    z = mix_fn(h32, hb)
    h32, hb = _layer_norm(z, ln_mix_g[layer], ln_mix_b[layer], tr=256, name=f"ln_mix_{layer}")
    hid = _matmul_w32([hb], w1, layer, tm=1024, tn=512, epilogue="relu2", name=f"mlp_up_{layer}")
    z = _matmul_ktiled(hid, w2, layer, tm=1024, tn=1024, tk=2048, epilogue="resid", extra=h32,
                       alpha=alpha, out_dtype=F32, name=f"mlp_down_{layer}")
    return _layer_norm(z, ln_ffn_g[layer], ln_ffn_b[layer], tr=256, name=f"ln_ffn_{layer}")


def kernel(x, even_w_in, ssm_a_re, ssm_a_im, ssm_log_dt, ssm_b_re, ssm_b_im, ssm_c_re, ssm_c_im, ssm_d,
           ssm_w_glu, even_w_out, fox_w_in, fox_b_f, fox_w_out, ln_mix_g, ln_mix_b, mlp_w1, mlp_w2,
           ln_ffn_g, ln_ffn_b):
    bsz, s, d_model = x.shape
    depth = ln_mix_g.shape[0]
    alpha = (2.0 * depth) ** 0.25
    m = bsz * s
    w_ssm = ssm_d.shape[-1]

    h32 = x.reshape(m, d_model)
    hb = h32.astype(BF16)
    mlp_w2_b = mlp_w2.astype(BF16)
    for layer in range(depth):
        i = layer // 2
        if layer % 2 == 0:
            def mix_fn(h32, hb, i=i):
                n_in = even_w_in.shape[-1]
                sb_heads = (n_in - w_ssm) // (3 * HEAD_DIM)
                proj = _matmul_w32([hb], even_w_in, i, tm=1024, tn=512, epilogue="colscale",
                                   extra=_query_colscale(n_in, w_ssm, sb_heads * HEAD_DIM), name="even_in")
                proj3 = proj.reshape(bsz, s, -1)
                bmat, cmat, ar, ai = _ssm_discretise(ssm_a_re[i], ssm_a_im[i], ssm_log_dt[i], ssm_b_re[i],
                                                     ssm_b_im[i], ssm_c_re[i], ssm_c_im[i])
                g = _ssm_scan(proj3, bmat, cmat, ar, ai, ssm_d[i], w_ssm=w_ssm, lc=256).reshape(m, w_ssm)
                y_ssm = _matmul_w32([g], ssm_w_glu, i, tm=1024, tn=512, epilogue="glu", extra=g, name="ssm_glu")
                y_sb = _sb_attention(proj3, heads=sb_heads, col0=w_ssm, tq=ATTN_TQ, tk=ATTN_TK).reshape(m, -1)
                return _matmul_w32([y_ssm, y_sb], even_w_out, i, tm=1024, tn=512, epilogue="resid",
                                   extra=h32, alpha=alpha, out_dtype=F32, name="even_out")
        else:
            def mix_fn(h32, hb, i=i):
                heads = fox_b_f.shape[-1]
                width = heads * HEAD_DIM
                w_nk = jnp.swapaxes(fox_w_in, 1, 2)
                qkv = _matmul_w32([hb], w_nk, i, tm=1024, tn=512, col_blocks=3 * width // 512, b_is_nk=True,
                                  epilogue="colscale", extra=_query_colscale(3 * width, 0, width), name="fox_in")
                qkv = qkv.reshape(bsz, s, 3 * width)
                w_f = jnp.pad(w_nk[i:i + 1, 3 * width:, :], ((0, 0), (0, LANES - heads), (0, 0)))
                b_f = jnp.pad(fox_b_f[i].astype(F32), (0, LANES - heads)).reshape(1, LANES)
                f_logit = _matmul_w32([hb], w_f, 0, tm=1024, tn=LANES, b_is_nk=True, out_dtype=F32,
                                      name="fox_forget")
                cum, kc = _forget_cumsum(f_logit.reshape(bsz, s, LANES), b_f, heads=heads)
                cum_t = cum[:, :, :heads].transpose(0, 2, 1)
                y = _fox_attention(qkv, kc, cum_t, heads=heads, tq=ATTN_TQ, tk=ATTN_TK)
                return _matmul_w32([y.reshape(m, width)], fox_w_out, i, tm=1024, tn=512, epilogue="resid",
                                   extra=h32, alpha=alpha, out_dtype=F32, name="fox_out")
        h32, hb = _mlp_and_norms(h32, hb, mix_fn, ln_mix_g, ln_mix_b, mlp_w1, mlp_w2_b, ln_ffn_g, ln_ffn_b,
                                 layer=layer, alpha=alpha)
    return h32.reshape(bsz, s, d_model)
```

```python
import functools
import math

import jax
import jax.numpy as jnp
from jax import lax
from jax.experimental import pallas as pl
from jax.experimental.pallas import tpu as pltpu

F32 = jnp.float32
BF16 = jnp.bfloat16

HEAD_DIM = 128
LN_EPS = 1e-5
LOG2E = math.log2(math.e)
LANES = 128
SUBLANES = 8
MXU_DIM = 256
VMEM_CAP = 56 * 1024 * 1024
CUM_PIECES = 3
EXP2_CLAMP = 126.0

_NT = (((1,), (1,)), ((), ()))
_TN = (((0,), (0,)), ((), ()))


def _vmem_limit(estimate_bytes):
    return int(min(VMEM_CAP, max(32 * 1024 * 1024, estimate_bytes * 5 // 4)))


def _log_sigmoid(x):
    return jnp.minimum(x, 0.0) - jnp.log(1.0 + jnp.exp(-jnp.abs(x)))


def _gelu_tanh(x):
    c = math.sqrt(2.0 / math.pi)
    return 0.5 * x * (1.0 + jnp.tanh(c * (x + 0.044715 * (x * x * x))))


def _epilogue(acc, e_ref, o_ref, epilogue, alpha):
    if epilogue == "cast":
        o_ref[...] = acc.astype(o_ref.dtype)
    elif epilogue == "colscale":
        o_ref[...] = (acc * e_ref[...]).astype(o_ref.dtype)
    elif epilogue == "relu2":
        r = jnp.maximum(acc, 0.0)
        o_ref[...] = (r * r).astype(o_ref.dtype)
    elif epilogue == "glu":
        g = e_ref[...].astype(F32)
        o_ref[...] = (g * (1.0 / (1.0 + jnp.exp(-acc)))).astype(o_ref.dtype)
    elif epilogue == "resid":
        o_ref[...] = (alpha * e_ref[...].astype(F32) + acc).astype(o_ref.dtype)
    else:
        raise ValueError(epilogue)


def _mm_w32_kernel(*refs, n_a, epilogue, alpha, has_extra, b_is_nk):
    a_refs = refs[:n_a]
    b_ref = refs[n_a]
    e_ref = refs[n_a + 1] if has_extra else None
    o_ref, wb_ref = refs[-2], refs[-1]

    @pl.when(pl.program_id(1) == 0)
    def _():
        w = b_ref[...].T if b_is_nk else b_ref[...]
        wb_ref[...] = w.astype(BF16)

    acc = None
    k0 = 0
    for a_ref in a_refs:
        kw = a_ref.shape[1]
        part = jnp.dot(a_ref[...], wb_ref[k0:k0 + kw, :], preferred_element_type=F32)
        acc = part if acc is None else acc + part
        k0 += kw
    _epilogue(acc, e_ref, o_ref, epilogue, alpha)


def _mm_ktiled_kernel(a_ref, b_ref, e_ref, o_ref, acc_ref, *, nk, epilogue, alpha):
    k = pl.program_id(2)

    @pl.when(k == 0)
    def _():
        acc_ref[...] = jnp.zeros_like(acc_ref)

    acc_ref[...] += jnp.dot(a_ref[...], b_ref[...], preferred_element_type=F32)

    @pl.when(k == nk - 1)
    def _():
        _epilogue(acc_ref[...], e_ref, o_ref, epilogue, alpha)


def _matmul_ktiled(a, b, layer, *, tm, tn, tk, epilogue, extra, alpha, out_dtype, name):
    m, kdim = a.shape
    _, k2, n = b.shape
    assert kdim == k2
    tm, tn, tk = min(tm, m), min(tn, n), min(tk, kdim)
    assert m % tm == 0 and n % tn == 0 and kdim % tk == 0
    nk = kdim // tk
    est = 2 * (tm * tk + tk * tn) * 2 + 2 * tm * tn * (jnp.dtype(out_dtype).itemsize + extra.dtype.itemsize)
    est += 2 * tm * tn * 4
    return pl.pallas_call(
        functools.partial(_mm_ktiled_kernel, nk=nk, epilogue=epilogue, alpha=alpha),
        grid=(m // tm, n // tn, nk),
        in_specs=[pl.BlockSpec((tm, tk), lambda i, j, k: (i, k)),
                  pl.BlockSpec((None, tk, tn), lambda i, j, k: (layer, k, j)),
                  pl.BlockSpec((tm, tn), lambda i, j, k: (i, j))],
        out_specs=pl.BlockSpec((tm, tn), lambda i, j, k: (i, j)),
        out_shape=jax.ShapeDtypeStruct((m, n), out_dtype),
        scratch_shapes=[pltpu.VMEM((tm, tn), F32)],
        compiler_params=pltpu.CompilerParams(
            dimension_semantics=("parallel", "parallel", "arbitrary"),
            vmem_limit_bytes=_vmem_limit(est)),
        name=name,
    )(a, b, extra)


def _matmul_w32(a_parts, b, layer, *, tm, tn, col_blocks=None, b_is_nk=False, epilogue="cast", extra=None,
                alpha=1.0, out_dtype=BF16, name):
    a_parts = tuple(a_parts)
    m = a_parts[0].shape[0]
    kdim, n_all = (b.shape[2], b.shape[1]) if b_is_nk else (b.shape[1], b.shape[2])
    assert sum(a.shape[1] for a in a_parts) == kdim
    tm, tn = min(tm, m), min(tn, n_all)
    nj = n_all // tn if col_blocks is None else col_blocks
    n = nj * tn
    assert m % tm == 0 and n <= n_all
    has_extra = extra is not None
    in_specs = [pl.BlockSpec((tm, a.shape[1]), lambda j, i: (i, 0)) for a in a_parts]
    if b_is_nk:
        in_specs.append(pl.BlockSpec((None, tn, kdim), lambda j, i: (layer, j, 0)))
    else:
        in_specs.append(pl.BlockSpec((None, kdim, tn), lambda j, i: (layer, 0, j)))
    args = list(a_parts) + [b]
    est = 2 * tm * kdim * 2 + 2 * kdim * tn * 4 + kdim * tn * 2
    est += 2 * tm * tn * jnp.dtype(out_dtype).itemsize + 2 * tm * tn * 4
    if has_extra:
        if epilogue == "colscale":
            in_specs.append(pl.BlockSpec((1, tn), lambda j, i: (0, j)))
        else:
            in_specs.append(pl.BlockSpec((tm, tn), lambda j, i: (i, j)))
            est += 2 * tm * tn * extra.dtype.itemsize
        args.append(extra)
    return pl.pallas_call(
        functools.partial(_mm_w32_kernel, n_a=len(a_parts), epilogue=epilogue, alpha=alpha,
                          has_extra=has_extra, b_is_nk=b_is_nk),
        grid=(nj, m // tm),
        in_specs=in_specs,
        out_specs=pl.BlockSpec((tm, tn), lambda j, i: (i, j)),
        out_shape=jax.ShapeDtypeStruct((m, n), out_dtype),
        scratch_shapes=[pltpu.VMEM((kdim, tn), BF16)],
        compiler_params=pltpu.CompilerParams(
            dimension_semantics=("parallel", "arbitrary"),
            vmem_limit_bytes=_vmem_limit(est)),
        name=name,
    )(*args)


def _query_colscale(n_cols, q_start, q_width):
    col = jnp.arange(n_cols)
    is_q = (col >= q_start) & (col < q_start + q_width)
    return jnp.where(is_q, HEAD_DIM ** -0.5 * LOG2E, 1.0).astype(F32).reshape(1, n_cols)


def _ln_kernel(x_ref, g_ref, b_ref, of_ref, ob_ref):
    x = x_ref[...]
    mu = jnp.mean(x, axis=-1, keepdims=True)
    xc = x - mu
    var = jnp.mean(xc * xc, axis=-1, keepdims=True)
    y = xc * lax.rsqrt(var + LN_EPS) * g_ref[...] + b_ref[...]
    of_ref[...] = y
    ob_ref[...] = y.astype(BF16)


def _layer_norm(x, g, b, *, tr, name):
    m, d = x.shape
    tr = min(tr, m)
    assert m % tr == 0
    row = pl.BlockSpec((tr, d), lambda i: (i, 0))
    vec = pl.BlockSpec((1, d), lambda i: (0, 0))
    return pl.pallas_call(
        _ln_kernel,
        grid=(m // tr,),
        in_specs=[row, vec, vec],
        out_specs=[row, row],
        out_shape=[jax.ShapeDtypeStruct((m, d), F32), jax.ShapeDtypeStruct((m, d), BF16)],
        compiler_params=pltpu.CompilerParams(
            dimension_semantics=("parallel",),
            vmem_limit_bytes=_vmem_limit(2 * tr * d * (4 + 4 + 2) + 4 * tr * d * 4)),
        name=name,
    )(x, g.reshape(1, d).astype(F32), b.reshape(1, d).astype(F32))


def _ssm_kernel(u_ref, bm_ref, cm_ref, ar_ref, ai_ref, d_ref, o_ref, sre, sim, st_ref,
                *, lc, pitch, nq, tiles_per_q, n_vregs):
    c = pl.program_id(1)

    @pl.when(c == 0)
    def _():
        st_ref[...] = jnp.zeros_like(st_ref)

    for q in range(nq):
        uq = u_ref[0, :, q * LANES:(q + 1) * LANES]
        res = jnp.dot(uq, bm_ref[q], preferred_element_type=F32)
        for tl in range(tiles_per_q):
            base = (q * tiles_per_q + tl) * pitch
            sre[base:base + lc, :] = res[:, (2 * tl) * LANES:(2 * tl + 1) * LANES]
            sim[base:base + lc, :] = res[:, (2 * tl + 1) * LANES:(2 * tl + 2) * LANES]

    ar = [ar_ref[j * SUBLANES:(j + 1) * SUBLANES, :] for j in range(n_vregs)]
    ai = [ai_ref[j * SUBLANES:(j + 1) * SUBLANES, :] for j in range(n_vregs)]
    xr0 = tuple(st_ref[0, j * SUBLANES:(j + 1) * SUBLANES, :] for j in range(n_vregs))
    xi0 = tuple(st_ref[1, j * SUBLANES:(j + 1) * SUBLANES, :] for j in range(n_vregs))

    def step(t, carry):
        xr, xi = carry
        nxr, nxi = [], []
        for j in range(n_vregs):
            rows = pl.ds(j * SUBLANES * pitch + t, SUBLANES, stride=pitch)
            nr = ar[j] * xr[j] - ai[j] * xi[j] + sre[rows, :]
            ni = ar[j] * xi[j] + ai[j] * xr[j] + sim[rows, :]
            sre[rows, :] = nr
            sim[rows, :] = ni
            nxr.append(nr)
            nxi.append(ni)
        return tuple(nxr), tuple(nxi)

    xr, xi = lax.fori_loop(0, lc, step, (xr0, xi0))
    for j in range(n_vregs):
        st_ref[0, j * SUBLANES:(j + 1) * SUBLANES, :] = xr[j]
        st_ref[1, j * SUBLANES:(j + 1) * SUBLANES, :] = xi[j]

    for q in range(nq):
        pieces = []
        for tl in range(tiles_per_q):
            base = (q * tiles_per_q + tl) * pitch
            pieces.append(sre[base:base + lc, :].astype(BF16))
            pieces.append(sim[base:base + lc, :].astype(BF16))
        xcat = jnp.concatenate(pieces, axis=1)
        y = jnp.dot(xcat, cm_ref[q], preferred_element_type=F32)
        uq = u_ref[0, :, q * LANES:(q + 1) * LANES].astype(F32)
        y = y + d_ref[:, q * LANES:(q + 1) * LANES] * uq
        o_ref[0, :, q * LANES:(q + 1) * LANES] = _gelu_tanh(y).astype(o_ref.dtype)


def _ssm_discretise(a_re, a_im, log_dt, b_re, b_im, c_re, c_im):
    g, n = a_re.shape
    p = b_re.shape[-1]
    lr = a_re.astype(F32)
    li = a_im.astype(F32)
    dt = jnp.exp(log_dt.astype(F32))[:, None]
    mag = jnp.exp(lr * dt)
    abar_r = mag * jnp.cos(li * dt)
    abar_i = mag * jnp.sin(li * dt)
    den = lr * lr + li * li
    nr = abar_r - 1.0
    ni = abar_i
    zr = (nr * lr + ni * li) / den
    zi = (ni * lr - nr * li) / den
    br = b_re.astype(F32)
    bi = b_im.astype(F32)
    bbar_r = zr[..., None] * br - zi[..., None] * bi
    bbar_i = zr[..., None] * bi + zi[..., None] * br

    gq = LANES // p
    gt = LANES // n
    nq = g // gq
    tq = gq // gt
    eye = jnp.eye(gq, dtype=F32)
    bb = jnp.stack([bbar_r, bbar_i], 0).reshape(2, nq, gq, n, p)
    bf = jnp.einsum("aqhnp,gh->qgpahn", bb, eye)
    bf = bf.reshape(nq, gq, p, 2, tq, gt, n).transpose(0, 1, 2, 4, 3, 5, 6)
    bmat = bf.reshape(nq, gq * p, tq * 2 * gt * n).astype(BF16)
    cc = jnp.stack([c_re.astype(F32), -c_im.astype(F32)], 0).reshape(2, nq, gq, p, n)
    cf = jnp.einsum("aqhpn,gh->qahngp", cc, eye)
    cf = cf.reshape(nq, 2, tq, gt, n, gq, p).transpose(0, 2, 1, 3, 4, 5, 6)
    cmat = cf.reshape(nq, tq * 2 * gt * n, gq * p).astype(BF16)
    n_tiles = g * n // LANES
    return bmat, cmat, abar_r.reshape(n_tiles, LANES), abar_i.reshape(n_tiles, LANES)


def _ssm_scan(proj, bmat, cmat, ar, ai, d, *, w_ssm, lc):
    bsz, s, _ = proj.shape
    lc = min(lc, s)
    assert s % lc == 0 and lc % SUBLANES == 0
    nq, _, qcols = bmat.shape
    tiles_per_q = qcols // (2 * LANES)
    n_tiles = ar.shape[0]
    assert n_tiles == nq * tiles_per_q and n_tiles % SUBLANES == 0 and w_ssm == nq * LANES
    n_vregs = n_tiles // SUBLANES
    pitch = lc + 4
    est = (2 * 2 * lc * w_ssm * 2 + 2 * (bmat.size + cmat.size) * 2
           + 2 * n_tiles * pitch * LANES * 4 + 4 * lc * qcols * 4)
    kern = functools.partial(_ssm_kernel, lc=lc, pitch=pitch, nq=nq, tiles_per_q=tiles_per_q,
                             n_vregs=n_vregs)
    const2 = lambda b, c: (0, 0)
    const3 = lambda b, c: (0, 0, 0)
    return pl.pallas_call(
        kern,
        grid=(bsz, s // lc),
        in_specs=[pl.BlockSpec((1, lc, w_ssm), lambda b, c: (b, c, 0)),
                  pl.BlockSpec(bmat.shape, const3),
                  pl.BlockSpec(cmat.shape, const3),
                  pl.BlockSpec(ar.shape, const2),
                  pl.BlockSpec(ai.shape, const2),
                  pl.BlockSpec((1, w_ssm), const2)],
        out_specs=pl.BlockSpec((1, lc, w_ssm), lambda b, c: (b, c, 0)),
        out_shape=jax.ShapeDtypeStruct((bsz, s, w_ssm), BF16),
        scratch_shapes=[pltpu.VMEM((n_tiles * pitch, LANES), F32),
                        pltpu.VMEM((n_tiles * pitch, LANES), F32),
                        pltpu.VMEM((2, n_tiles, LANES), F32)],
        compiler_params=pltpu.CompilerParams(
            dimension_semantics=("parallel", "arbitrary"),
            vmem_limit_bytes=_vmem_limit(est)),
        name="s5_scan",
    )(proj, bmat, cmat, ar, ai, d.reshape(1, w_ssm).astype(F32))


def _attn_tiles(s, tq, tk):
    tq, tk = min(tq, s), min(tk, s)
    assert s % tq == 0 and tq % (2 * tk) == 0
    return tq, tk


def _key_query_iota(tk, tq):
    key = lax.broadcasted_iota(jnp.int32, (tk, tq), 0)
    query = lax.broadcasted_iota(jnp.int32, (tk, tq), 1)
    return key, query


def _transpose_values(v_ref, vt_ref, hs, chunk=512):
    s = v_ref.shape[1]
    chunk = min(chunk, s)

    @pl.when(pl.program_id(2) == 0)
    def _():
        for h in range(hs):
            for c in range(s // chunk):
                rows = slice(c * chunk, (c + 1) * chunk)
                vt_ref[h, :, rows] = v_ref[0, rows, h * HEAD_DIM:(h + 1) * HEAD_DIM].T


def _sb_kernel(q_ref, k_ref, v_ref, o_ref, z_ref, hi_ref, lo_ref, acc_ref, vt_ref, carry_ref, *, tq, tk, hs):
    qi = pl.program_id(2)
    r = tq // tk
    heads = range(hs)
    kk = lax.broadcasted_iota(jnp.int32, (tk, tk), 0)
    kk2 = lax.broadcasted_iota(jnp.int32, (tk, tk), 1)
    at_or_after = (kk2 >= kk).astype(BF16)

    def head_cols(h):
        return slice(h * HEAD_DIM, (h + 1) * HEAD_DIM)

    q = [q_ref[0, :, head_cols(h)] for h in heads]
    _transpose_values(v_ref, vt_ref, hs)

    def keep_logs(h, slot, kb, q0=0, mask=None):
        k = k_ref[0, pl.ds(pl.multiple_of(kb * tk, tk), tk), head_cols(h)]
        z = lax.dot_general(k, q[h][q0:, :], _NT, preferred_element_type=F32)
        nk = jnp.maximum(z, jnp.log2(1.0 + jnp.exp2(jnp.minimum(z, EXP2_CLAMP))))
        if mask is not None:
            nk = jnp.where(mask, nk, 0.0)
        hi = nk.astype(BF16)
        z_ref[h, slot, :, q0:] = z
        hi_ref[h, slot, :, q0:] = hi
        lo_ref[h, slot, :, q0:] = (nk - hi.astype(F32)).astype(BF16)

    def accumulate(h, slot, kb, carry, q0=0, mask=None):
        vt = vt_ref[h, :, pl.ds(pl.multiple_of(kb * tk, tk), tk)]
        incl = (jnp.dot(at_or_after, hi_ref[h, slot, :, q0:], preferred_element_type=F32)
                + jnp.dot(at_or_after, lo_ref[h, slot, :, q0:], preferred_element_type=F32))
        w = jnp.exp2(z_ref[h, slot, :, q0:] - incl - carry)
        if mask is not None:
            w = jnp.where(mask, w, 0.0)
        acc_ref[h, :, q0:] += jnp.dot(vt, w.astype(BF16), preferred_element_type=F32)
        return carry + incl[0:1, :]

    carry_ref[...] = jnp.zeros_like(carry_ref)
    acc_ref[...] = jnp.zeros_like(acc_ref)
    for d in reversed(range(r)):
        q0 = d * tk
        key, query = _key_query_iota(tk, tq - q0)
        mask = key < query
        for h in heads:
            keep_logs(h, d % 2, qi * r + d, q0, mask)
        for h in heads:
            carry_ref[h, :, q0:] = accumulate(h, d % 2, qi * r + d, carry_ref[h, :, q0:], q0, mask)
    carry = [carry_ref[h] for h in heads]

    n_full = qi * r
    for h in heads:
        keep_logs(h, 0, jnp.maximum(n_full - 1, 0))

    def body(it, carry):
        kb = n_full - 1 - 2 * it
        for h in heads:
            keep_logs(h, 1, kb - 1)
        carry = [accumulate(h, 0, kb, carry[h]) for h in heads]
        for h in heads:
            keep_logs(h, 0, jnp.maximum(kb - 2, 0))
        return [accumulate(h, 1, kb - 1, carry[h]) for h in heads]

    lax.fori_loop(0, n_full // 2, body, carry)
    for h in heads:
        o_ref[0, :, head_cols(h)] = acc_ref[h].T.astype(o_ref.dtype)


def _sb_attention(proj, *, heads, col0, tq, tk):
    bsz, s, _ = proj.shape
    tq, tk = _attn_tiles(s, tq, tk)
    hs = min(SB_HEADS_PER_STEP, heads)
    wcols = hs * HEAD_DIM
    assert heads % hs == 0 and col0 % wcols == 0 and (heads * HEAD_DIM) % wcols == 0
    c0, hg = col0 // wcols, heads // hs
    est = 2 * (2 * s * wcols * 2) + 4 * tq * wcols * 2 + hs * 16 * tq * tk * 4
    return pl.pallas_call(
        functools.partial(_sb_kernel, tq=tq, tk=tk, hs=hs),
        grid=(bsz, hg, s // tq),
        in_specs=[pl.BlockSpec((1, tq, wcols), lambda b, h, i: (b, i, c0 + h)),
                  pl.BlockSpec((1, s, wcols), lambda b, h, i: (b, 0, c0 + hg + h)),
                  pl.BlockSpec((1, s, wcols), lambda b, h, i: (b, 0, c0 + 2 * hg + h))],
        out_specs=pl.BlockSpec((1, tq, wcols), lambda b, h, i: (b, i, h)),
        out_shape=jax.ShapeDtypeStruct((bsz, s, heads * HEAD_DIM), BF16),
        scratch_shapes=[pltpu.VMEM((hs, 2, tk, tq), F32), pltpu.VMEM((hs, 2, tk, tq), BF16),
                        pltpu.VMEM((hs, 2, tk, tq), BF16), pltpu.VMEM((hs, HEAD_DIM, tq), F32),
                        pltpu.VMEM((hs, HEAD_DIM, s), BF16), pltpu.VMEM((hs, 1, tq), F32)],
        compiler_params=pltpu.CompilerParams(
            dimension_semantics=("parallel", "parallel", "arbitrary"),
            vmem_limit_bytes=_vmem_limit(est)),
        name="stickbreak_attn",
    )(proj, proj, proj)


def _cum_kernel(f_ref, b_ref, o_ref, kc_ref, carry_ref, *, chunk, heads):
    ts, w = f_ref.shape[1], f_ref.shape[2]

    @pl.when(pl.program_id(1) == 0)
    def _():
        carry_ref[...] = jnp.zeros_like(carry_ref)

    row = lax.broadcasted_iota(jnp.int32, (chunk, chunk), 0)
    col = lax.broadcasted_iota(jnp.int32, (chunk, chunk), 1)
    tri = (col <= row).astype(F32)
    src = lax.broadcasted_iota(jnp.int32, (CUM_PIECES * w, heads * HEAD_DIM), 0)
    dst = lax.broadcasted_iota(jnp.int32, (CUM_PIECES * w, heads * HEAD_DIM), 1)
    place = ((src % w) * HEAD_DIM + src // w == dst).astype(BF16)
    carry = carry_ref[...]
    for c in range(ts // chunk):
        rows = slice(c * chunk, (c + 1) * chunk)
        ls = _log_sigmoid(f_ref[0, rows, :] + b_ref[...])
        cs = jnp.dot(tri, ls, preferred_element_type=F32, precision=lax.Precision.HIGHEST) + carry
        carry = cs[chunk - 1:chunk, :]
        o_ref[0, rows, :] = cs * LOG2E
    carry_ref[...] = carry
    rest = -o_ref[0]
    pieces = []
    for _ in range(CUM_PIECES):
        piece = rest.astype(BF16)
        pieces.append(piece)
        rest = rest - piece.astype(F32)
    kc_ref[0] = jnp.dot(jnp.concatenate(pieces, axis=1), place, preferred_element_type=F32).astype(BF16)


def _forget_cumsum(f_logit, b_f, *, heads, chunk=256, ts=512):
    bsz, s, w = f_logit.shape
    ts = min(ts, s)
    chunk = min(chunk, ts)
    assert s % ts == 0 and ts % chunk == 0 and heads <= w
    return pl.pallas_call(
        functools.partial(_cum_kernel, chunk=chunk, heads=heads),
        grid=(bsz, s // ts),
        in_specs=[pl.BlockSpec((1, ts, w), lambda b, t: (b, t, 0)), pl.BlockSpec((1, w), lambda b, t: (0, 0))],
        out_specs=[pl.BlockSpec((1, ts, w), lambda b, t: (b, t, 0)),
                   pl.BlockSpec((1, ts, heads * HEAD_DIM), lambda b, t: (b, t, 0))],
        out_shape=[jax.ShapeDtypeStruct((bsz, s, w), F32),
                   jax.ShapeDtypeStruct((bsz, s, heads * HEAD_DIM), BF16)],
        scratch_shapes=[pltpu.VMEM((1, w), F32)],
        compiler_params=pltpu.CompilerParams(dimension_semantics=("parallel", "arbitrary")),
        name="forget_cumsum",
    )(f_logit, b_f)


def _fox_kernel(q_ref, k_ref, kc_ref, v_ref, ci_ref, o_ref, z_ref, p_ref, acc_ref, vt_ref, st_ref,
                *, tq, tk, hs):
    qi = pl.program_id(2)
    r = tq // tk
    heads = range(hs)
    lane = lax.broadcasted_iota(jnp.int32, (tq, HEAD_DIM), 1)
    ones = (lane < CUM_PIECES).astype(BF16)

    def head_cols(h):
        return slice(h * HEAD_DIM, (h + 1) * HEAD_DIM)

    q = [jnp.concatenate([q_ref[0, :, head_cols(h)], ones], axis=1) for h in heads]
    _transpose_values(v_ref, vt_ref, hs)

    def logits(h, kb, q0=0):
        rows = pl.ds(pl.multiple_of(kb * tk, tk), tk)
        k = jnp.concatenate([k_ref[0, rows, head_cols(h)], kc_ref[0, rows, head_cols(h)]], axis=1)
        return lax.dot_general(k, q[h][q0:, :], _NT, preferred_element_type=F32)

    def softmax(h, z, st, q0=0):
        _, m, l = st
        cq = ci_ref[0, h, pl.ds(qi, 1), q0:]
        m_new = jnp.maximum(m, jnp.max(z, axis=0, keepdims=True) + cq)
        p = jnp.exp2(z - (m_new - cq))
        a = jnp.exp2(m - m_new)
        return p.astype(BF16), (a, m_new, a * l + jnp.sum(p, axis=0, keepdims=True))

    def rescale_add(h, st, p, kb):
        vt = vt_ref[h, :, pl.ds(pl.multiple_of(kb * tk, tk), tk)]
        return st[0] * acc_ref[h] + jnp.dot(vt, p, preferred_element_type=F32)

    st_ref[:, 0] = jnp.ones_like(st_ref[:, 0])
    st_ref[:, 1] = jnp.full_like(st_ref[:, 1], -jnp.inf)
    st_ref[:, 2] = jnp.zeros_like(st_ref[:, 2])
    acc_ref[...] = jnp.zeros_like(acc_ref)
    for d in range(r):
        q0 = d * tk
        key, query = _key_query_iota(tk, tq - q0)
        mask = key <= query
        z = [jnp.where(mask, logits(h, qi * r + d, q0), -jnp.inf) for h in heads]
        if d > 0:
            for h in heads:
                acc_ref[h] = rescale_add(h, (st_ref[h, 0],), p_ref[h, (d - 1) % 2], qi * r + d - 1)
        for h in heads:
            if q0:
                p_ref[h, d % 2, :, :q0] = jnp.zeros((tk, q0), BF16)
                st_ref[h, 0, :, :q0] = jnp.ones((1, q0), F32)
            p, (a, m, l) = softmax(h, z[h], (None, st_ref[h, 1, :, q0:], st_ref[h, 2, :, q0:]), q0)
            p_ref[h, d % 2, :, q0:] = p
            st_ref[h, 0, :, q0:] = a
            st_ref[h, 1, :, q0:] = m
            st_ref[h, 2, :, q0:] = l
    st = [(st_ref[h, 0], st_ref[h, 1], st_ref[h, 2]) for h in heads]

    n_full = qi * r
    for h in heads:
        z_ref[h, 0] = logits(h, jnp.maximum(n_full - 1, 0))

    def step(slot, kb, kb_p, st):
        st = list(st)
        for h in heads:
            z_ref[h, 1 - slot] = logits(h, jnp.maximum(kb - 1, 0))
        for h in heads:
            acc_ref[h] = rescale_add(h, st[h], p_ref[h, 1 - slot], kb_p)
        for h in heads:
            p_ref[h, slot], st[h] = softmax(h, z_ref[h, slot], st[h])
        return st

    def body(it, st):
        kb = n_full - 1 - 2 * it
        st = step(0, kb, jnp.where(it == 0, qi * r + r - 1, kb + 1), st)
        return step(1, kb - 1, kb, st)

    st = lax.fori_loop(0, n_full // 2, body, st)
    for h in heads:
        acc = rescale_add(h, st[h], p_ref[h, 1], jnp.where(qi == 0, r - 1, 0))
        o_ref[0, :, head_cols(h)] = (acc / st[h][2]).T.astype(o_ref.dtype)


def _fox_attention(qkv, kc, cum_t, *, heads, tq, tk):
    bsz, s, _ = qkv.shape
    tq, tk = _attn_tiles(s, tq, tk)
    nq = s // tq
    ci = cum_t.reshape(bsz, heads, nq, tq)
    hs = min(FOX_HEADS_PER_STEP, heads)
    wcols = hs * HEAD_DIM
    assert heads % hs == 0
    hg = heads // hs
    est = 2 * (3 * s * wcols * 2) + 4 * tq * wcols * 2 + hs * 16 * tq * tk * 4
    return pl.pallas_call(
        functools.partial(_fox_kernel, tq=tq, tk=tk, hs=hs),
        grid=(bsz, hg, nq),
        in_specs=[pl.BlockSpec((1, tq, wcols), lambda b, h, i: (b, i, h)),
                  pl.BlockSpec((1, s, wcols), lambda b, h, i: (b, 0, hg + h)),
                  pl.BlockSpec((1, s, wcols), lambda b, h, i: (b, 0, h)),
                  pl.BlockSpec((1, s, wcols), lambda b, h, i: (b, 0, 2 * hg + h)),
                  pl.BlockSpec((1, hs, nq, tq), lambda b, h, i: (b, h, 0, 0))],
        out_specs=pl.BlockSpec((1, tq, wcols), lambda b, h, i: (b, i, h)),
        out_shape=jax.ShapeDtypeStruct((bsz, s, heads * HEAD_DIM), BF16),
        scratch_shapes=[pltpu.VMEM((hs, 2, tk, tq), F32), pltpu.VMEM((hs, 2, tk, tq), BF16),
                        pltpu.VMEM((hs, HEAD_DIM, tq), F32), pltpu.VMEM((hs, HEAD_DIM, s), BF16),
                        pltpu.VMEM((hs, 3, 1, tq), F32)],
        compiler_params=pltpu.CompilerParams(
            dimension_semantics=("parallel", "parallel", "arbitrary"),
            vmem_limit_bytes=_vmem_limit(est)),
        name="fox_attn",
    )(qkv, qkv, kc, qkv, ci)


SB_TQ, SB_TK, SB_HEADS_PER_STEP = 1024, 256, 4
FOX_TQ, FOX_TK, FOX_HEADS_PER_STEP = 512, 256, 4


def _mlp_and_norms(h32, hb, mix_fn, ln_mix_g, ln_mix_b, w1, w2, ln_ffn_g, ln_ffn_b, *, layer, alpha):
    z = mix_fn(h32, hb)
    h32, hb = _layer_norm(z, ln_mix_g[layer], ln_mix_b[layer], tr=256, name=f"ln_mix_{layer}")
    hid = _matmul_w32([hb], w1, layer, tm=1024, tn=512, epilogue="relu2", name=f"mlp_up_{layer}")
    z = _matmul_ktiled(hid, w2, layer, tm=1024, tn=1024, tk=2048, epilogue="resid", extra=h32,
                       alpha=alpha, out_dtype=F32, name=f"mlp_down_{layer}")
    return _layer_norm(z, ln_ffn_g[layer], ln_ffn_b[layer], tr=256, name=f"ln_ffn_{layer}")


def kernel(x, even_w_in, ssm_a_re, ssm_a_im, ssm_log_dt, ssm_b_re, ssm_b_im, ssm_c_re, ssm_c_im, ssm_d,
           ssm_w_glu, even_w_out, fox_w_in, fox_b_f, fox_w_out, ln_mix_g, ln_mix_b, mlp_w1, mlp_w2,
           ln_ffn_g, ln_ffn_b):
    bsz, s, d_model = x.shape
    depth = ln_mix_g.shape[0]
    alpha = (2.0 * depth) ** 0.25
    m = bsz * s
    w_ssm = ssm_d.shape[-1]

    h32 = x.reshape(m, d_model)
    hb = h32.astype(BF16)
    mlp_w2_b = mlp_w2.astype(BF16)
    for layer in range(depth):
        i = layer // 2
        if layer % 2 == 0:
            def mix_fn(h32, hb, i=i):
                n_in = even_w_in.shape[-1]
                sb_heads = (n_in - w_ssm) // (3 * HEAD_DIM)
                proj = _matmul_w32([hb], even_w_in, i, tm=1024, tn=512, epilogue="colscale",
                                   extra=_query_colscale(n_in, w_ssm, sb_heads * HEAD_DIM), name="even_in")
                proj3 = proj.reshape(bsz, s, -1)
                bmat, cmat, ar, ai = _ssm_discretise(ssm_a_re[i], ssm_a_im[i], ssm_log_dt[i], ssm_b_re[i],
                                                     ssm_b_im[i], ssm_c_re[i], ssm_c_im[i])
                g = _ssm_scan(proj3, bmat, cmat, ar, ai, ssm_d[i], w_ssm=w_ssm, lc=256).reshape(m, w_ssm)
                y_ssm = _matmul_w32([g], ssm_w_glu, i, tm=1024, tn=512, epilogue="glu", extra=g, name="ssm_glu")
                y_sb = _sb_attention(proj3, heads=sb_heads, col0=w_ssm, tq=SB_TQ, tk=SB_TK).reshape(m, -1)
                return _matmul_w32([y_ssm, y_sb], even_w_out, i, tm=1024, tn=512, epilogue="resid",
                                   extra=h32, alpha=alpha, out_dtype=F32, name="even_out")
        else:
            def mix_fn(h32, hb, i=i):
                heads = fox_b_f.shape[-1]
                width = heads * HEAD_DIM
                w_nk = jnp.swapaxes(fox_w_in, 1, 2)
                qkv = _matmul_w32([hb], w_nk, i, tm=1024, tn=512, col_blocks=3 * width // 512, b_is_nk=True,
                                  epilogue="colscale", extra=_query_colscale(3 * width, 0, width), name="fox_in")
                qkv = qkv.reshape(bsz, s, 3 * width)
                w_f = jnp.pad(w_nk[i:i + 1, 3 * width:, :], ((0, 0), (0, LANES - heads), (0, 0)))
                b_f = jnp.pad(fox_b_f[i].astype(F32), (0, LANES - heads)).reshape(1, LANES)
                f_logit = _matmul_w32([hb], w_f, 0, tm=1024, tn=LANES, b_is_nk=True, out_dtype=F32,
                                      name="fox_forget")
                cum, kc = _forget_cumsum(f_logit.reshape(bsz, s, LANES), b_f, heads=heads)
                cum_t = cum[:, :, :heads].transpose(0, 2, 1)
                y = _fox_attention(qkv, kc, cum_t, heads=heads, tq=FOX_TQ, tk=FOX_TK)
                return _matmul_w32([y.reshape(m, width)], fox_w_out, i, tm=1024, tn=512, epilogue="resid",
                                   extra=h32, alpha=alpha, out_dtype=F32, name="fox_out")
        h32, hb = _mlp_and_norms(h32, hb, mix_fn, ln_mix_g, ln_mix_b, mlp_w1, mlp_w2_b, ln_ffn_g, ln_ffn_b,
                                 layer=layer, alpha=alpha)
    return h32.reshape(bsz, s, d_model)
```

```python
import functools
import math

import jax
import jax.numpy as jnp
from jax import lax
from jax.experimental import pallas as pl
from jax.experimental.pallas import tpu as pltpu

F32 = jnp.float32
BF16 = jnp.bfloat16

HEAD_DIM = 128
LN_EPS = 1e-5
LOG2E = math.log2(math.e)
LANES = 128
SUBLANES = 8
VMEM_CAP = 56 * 1024 * 1024
CUM_PIECES = 3
EXP2_CLAMP = 126.0

_NT = (((1,), (1,)), ((), ()))


def _vmem_limit(estimate_bytes):
    return int(min(VMEM_CAP, max(32 * 1024 * 1024, estimate_bytes * 5 // 4)))


def _log_sigmoid(x):
    return jnp.minimum(x, 0.0) - jnp.log(1.0 + jnp.exp(-jnp.abs(x)))


def _gelu_tanh(x):
    c = math.sqrt(2.0 / math.pi)
    return 0.5 * x * (1.0 + jnp.tanh(c * (x + 0.044715 * (x * x * x))))


def _epilogue(acc, e_ref, o_ref, epilogue, alpha):
    if epilogue == "cast":
        o_ref[...] = acc.astype(o_ref.dtype)
    elif epilogue == "colscale":
        o_ref[...] = (acc * e_ref[...]).astype(o_ref.dtype)
    elif epilogue == "relu2":
        r = jnp.maximum(acc, 0.0)
        o_ref[...] = (r * r).astype(o_ref.dtype)
    elif epilogue == "glu":
        g = e_ref[...].astype(F32)
        o_ref[...] = (g * (1.0 / (1.0 + jnp.exp(-acc)))).astype(o_ref.dtype)
    elif epilogue == "resid":
        o_ref[...] = (alpha * e_ref[...].astype(F32) + acc).astype(o_ref.dtype)
    else:
        raise ValueError(epilogue)


def _mm_w32_kernel(*refs, n_a, epilogue, alpha, has_extra, b_is_nk):
    a_refs = refs[:n_a]
    b_ref = refs[n_a]
    e_ref = refs[n_a + 1] if has_extra else None
    o_ref, wb_ref = refs[-2], refs[-1]

    @pl.when(pl.program_id(1) == 0)
    def _():
        w = b_ref[...].T if b_is_nk else b_ref[...]
        wb_ref[...] = w.astype(BF16)

    acc = None
    k0 = 0
    for a_ref in a_refs:
        kw = a_ref.shape[1]
        part = jnp.dot(a_ref[...], wb_ref[k0:k0 + kw, :], preferred_element_type=F32)
        acc = part if acc is None else acc + part
        k0 += kw
    _epilogue(acc, e_ref, o_ref, epilogue, alpha)


def _mm_ktiled_kernel(a_ref, b_ref, e_ref, o_ref, acc_ref, *, nk, epilogue, alpha):
    k = pl.program_id(2)

    @pl.when(k == 0)
    def _():
        acc_ref[...] = jnp.zeros_like(acc_ref)

    acc_ref[...] += jnp.dot(a_ref[...], b_ref[...], preferred_element_type=F32)

    @pl.when(k == nk - 1)
    def _():
        _epilogue(acc_ref[...], e_ref, o_ref, epilogue, alpha)


def _matmul_ktiled(a, b, layer, *, tm, tn, tk, epilogue, extra, alpha, out_dtype, name):
    m, kdim = a.shape
    _, k2, n = b.shape
    assert kdim == k2
    tm, tn, tk = min(tm, m), min(tn, n), min(tk, kdim)
    assert m % tm == 0 and n % tn == 0 and kdim % tk == 0
    nk = kdim // tk
    est = 2 * (tm * tk + tk * tn) * 2 + 2 * tm * tn * (jnp.dtype(out_dtype).itemsize + extra.dtype.itemsize)
    est += 2 * tm * tn * 4
    return pl.pallas_call(
        functools.partial(_mm_ktiled_kernel, nk=nk, epilogue=epilogue, alpha=alpha),
        grid=(m // tm, n // tn, nk),
        in_specs=[pl.BlockSpec((tm, tk), lambda i, j, k: (i, k)),
                  pl.BlockSpec((None, tk, tn), lambda i, j, k: (layer, k, j)),
                  pl.BlockSpec((tm, tn), lambda i, j, k: (i, j))],
        out_specs=pl.BlockSpec((tm, tn), lambda i, j, k: (i, j)),
        out_shape=jax.ShapeDtypeStruct((m, n), out_dtype),
        scratch_shapes=[pltpu.VMEM((tm, tn), F32)],
        compiler_params=pltpu.CompilerParams(
            dimension_semantics=("parallel", "parallel", "arbitrary"),
            vmem_limit_bytes=_vmem_limit(est)),
        name=name,
    )(a, b, extra)


def _matmul_w32(a_parts, b, layer, *, tm, tn, col_blocks=None, b_is_nk=False, epilogue="cast", extra=None,
                alpha=1.0, out_dtype=BF16, name):
    a_parts = tuple(a_parts)
    m = a_parts[0].shape[0]
    kdim, n_all = (b.shape[2], b.shape[1]) if b_is_nk else (b.shape[1], b.shape[2])
    assert sum(a.shape[1] for a in a_parts) == kdim
    tm, tn = min(tm, m), min(tn, n_all)
    nj = n_all // tn if col_blocks is None else col_blocks
    n = nj * tn
    assert m % tm == 0 and n <= n_all
    has_extra = extra is not None
    in_specs = [pl.BlockSpec((tm, a.shape[1]), lambda j, i: (i, 0)) for a in a_parts]
    if b_is_nk:
        in_specs.append(pl.BlockSpec((None, tn, kdim), lambda j, i: (layer, j, 0)))
    else:
        in_specs.append(pl.BlockSpec((None, kdim, tn), lambda j, i: (layer, 0, j)))
    args = list(a_parts) + [b]
    est = 2 * tm * kdim * 2 + 2 * kdim * tn * 4 + kdim * tn * 2
    est += 2 * tm * tn * jnp.dtype(out_dtype).itemsize + 2 * tm * tn * 4
    if has_extra:
        if epilogue == "colscale":
            in_specs.append(pl.BlockSpec((1, tn), lambda j, i: (0, j)))
        else:
            in_specs.append(pl.BlockSpec((tm, tn), lambda j, i: (i, j)))
            est += 2 * tm * tn * extra.dtype.itemsize
        args.append(extra)
    return pl.pallas_call(
        functools.partial(_mm_w32_kernel, n_a=len(a_parts), epilogue=epilogue, alpha=alpha,
                          has_extra=has_extra, b_is_nk=b_is_nk),
        grid=(nj, m // tm),
        in_specs=in_specs,
        out_specs=pl.BlockSpec((tm, tn), lambda j, i: (i, j)),
        out_shape=jax.ShapeDtypeStruct((m, n), out_dtype),
        scratch_shapes=[pltpu.VMEM((kdim, tn), BF16)],
        compiler_params=pltpu.CompilerParams(
            dimension_semantics=("parallel", "arbitrary"),
            vmem_limit_bytes=_vmem_limit(est)),
        name=name,
    )(*args)


def _query_colscale(n_cols, q_start, q_width):
    col = jnp.arange(n_cols)
    is_q = (col >= q_start) & (col < q_start + q_width)
    return jnp.where(is_q, HEAD_DIM ** -0.5 * LOG2E, 1.0).astype(F32).reshape(1, n_cols)


def _ln_kernel(x_ref, g_ref, b_ref, of_ref, ob_ref):
    x = x_ref[...]
    mu = jnp.mean(x, axis=-1, keepdims=True)
    xc = x - mu
    var = jnp.mean(xc * xc, axis=-1, keepdims=True)
    y = xc * lax.rsqrt(var + LN_EPS) * g_ref[...] + b_ref[...]
    of_ref[...] = y
    ob_ref[...] = y.astype(BF16)


def _layer_norm(x, g, b, *, tr, name):
    m, d = x.shape
    tr = min(tr, m)
    assert m % tr == 0
    row = pl.BlockSpec((tr, d), lambda i: (i, 0))
    vec = pl.BlockSpec((1, d), lambda i: (0, 0))
    return pl.pallas_call(
        _ln_kernel,
        grid=(m // tr,),
        in_specs=[row, vec, vec],
        out_specs=[row, row],
        out_shape=[jax.ShapeDtypeStruct((m, d), F32), jax.ShapeDtypeStruct((m, d), BF16)],
        compiler_params=pltpu.CompilerParams(
            dimension_semantics=("parallel",),
            vmem_limit_bytes=_vmem_limit(2 * tr * d * (4 + 4 + 2) + 4 * tr * d * 4)),
        name=name,
    )(x, g.reshape(1, d).astype(F32), b.reshape(1, d).astype(F32))


def _ssm_kernel(u_ref, bm_ref, cm_ref, ar_ref, ai_ref, d_ref, o_ref, sre, sim, st_ref,
                *, lc, pitch, nq, tiles_per_q, n_vregs):
    c = pl.program_id(1)

    @pl.when(c == 0)
    def _():
        st_ref[...] = jnp.zeros_like(st_ref)

    for q in range(nq):
        uq = u_ref[0, :, q * LANES:(q + 1) * LANES]
        res = jnp.dot(uq, bm_ref[q], preferred_element_type=F32)
        for tl in range(tiles_per_q):
            base = (q * tiles_per_q + tl) * pitch
            sre[base:base + lc, :] = res[:, (2 * tl) * LANES:(2 * tl + 1) * LANES]
            sim[base:base + lc, :] = res[:, (2 * tl + 1) * LANES:(2 * tl + 2) * LANES]

    ar = [ar_ref[j * SUBLANES:(j + 1) * SUBLANES, :] for j in range(n_vregs)]
    ai = [ai_ref[j * SUBLANES:(j + 1) * SUBLANES, :] for j in range(n_vregs)]
    xr0 = tuple(st_ref[0, j * SUBLANES:(j + 1) * SUBLANES, :] for j in range(n_vregs))
    xi0 = tuple(st_ref[1, j * SUBLANES:(j + 1) * SUBLANES, :] for j in range(n_vregs))

    def step(t, carry):
        xr, xi = carry
        nxr, nxi = [], []
        for j in range(n_vregs):
            rows = pl.ds(j * SUBLANES * pitch + t, SUBLANES, stride=pitch)
            nr = ar[j] * xr[j] - ai[j] * xi[j] + sre[rows, :]
            ni = ar[j] * xi[j] + ai[j] * xr[j] + sim[rows, :]
            sre[rows, :] = nr
            sim[rows, :] = ni
            nxr.append(nr)
            nxi.append(ni)
        return tuple(nxr), tuple(nxi)

    xr, xi = lax.fori_loop(0, lc, step, (xr0, xi0))
    for j in range(n_vregs):
        st_ref[0, j * SUBLANES:(j + 1) * SUBLANES, :] = xr[j]
        st_ref[1, j * SUBLANES:(j + 1) * SUBLANES, :] = xi[j]

    for q in range(nq):
        pieces = []
        for tl in range(tiles_per_q):
            base = (q * tiles_per_q + tl) * pitch
            pieces.append(sre[base:base + lc, :].astype(BF16))
            pieces.append(sim[base:base + lc, :].astype(BF16))
        xcat = jnp.concatenate(pieces, axis=1)
        y = jnp.dot(xcat, cm_ref[q], preferred_element_type=F32)
        uq = u_ref[0, :, q * LANES:(q + 1) * LANES].astype(F32)
        y = y + d_ref[:, q * LANES:(q + 1) * LANES] * uq
        o_ref[0, :, q * LANES:(q + 1) * LANES] = _gelu_tanh(y).astype(o_ref.dtype)


def _ssm_discretise(a_re, a_im, log_dt, b_re, b_im, c_re, c_im):
    g, n = a_re.shape
    p = b_re.shape[-1]
    lr = a_re.astype(F32)
    li = a_im.astype(F32)
    dt = jnp.exp(log_dt.astype(F32))[:, None]
    mag = jnp.exp(lr * dt)
    abar_r = mag * jnp.cos(li * dt)
    abar_i = mag * jnp.sin(li * dt)
    den = lr * lr + li * li
    nr = abar_r - 1.0
    ni = abar_i
    zr = (nr * lr + ni * li) / den
    zi = (ni * lr - nr * li) / den
    br = b_re.astype(F32)
    bi = b_im.astype(F32)
    bbar_r = zr[..., None] * br - zi[..., None] * bi
    bbar_i = zr[..., None] * bi + zi[..., None] * br

    gq = LANES // p
    gt = LANES // n
    nq = g // gq
    tq = gq // gt
    eye = jnp.eye(gq, dtype=F32)
    bb = jnp.stack([bbar_r, bbar_i], 0).reshape(2, nq, gq, n, p)
    bf = jnp.einsum("aqhnp,gh->qgpahn", bb, eye)
    bf = bf.reshape(nq, gq, p, 2, tq, gt, n).transpose(0, 1, 2, 4, 3, 5, 6)
    bmat = bf.reshape(nq, gq * p, tq * 2 * gt * n).astype(BF16)
    cc = jnp.stack([c_re.astype(F32), -c_im.astype(F32)], 0).reshape(2, nq, gq, p, n)
    cf = jnp.einsum("aqhpn,gh->qahngp", cc, eye)
    cf = cf.reshape(nq, 2, tq, gt, n, gq, p).transpose(0, 2, 1, 3, 4, 5, 6)
    cmat = cf.reshape(nq, tq * 2 * gt * n, gq * p).astype(BF16)
    n_tiles = g * n // LANES
    return bmat, cmat, abar_r.reshape(n_tiles, LANES), abar_i.reshape(n_tiles, LANES)


def _ssm_scan(proj, bmat, cmat, ar, ai, d, *, w_ssm, lc):
    bsz, s, _ = proj.shape
    lc = min(lc, s)
    assert s % lc == 0 and lc % SUBLANES == 0
    nq, _, qcols = bmat.shape
    tiles_per_q = qcols // (2 * LANES)
    n_tiles = ar.shape[0]
    assert n_tiles == nq * tiles_per_q and n_tiles % SUBLANES == 0 and w_ssm == nq * LANES
    n_vregs = n_tiles // SUBLANES
    pitch = lc + 4
    est = (2 * 2 * lc * w_ssm * 2 + 2 * (bmat.size + cmat.size) * 2
           + 2 * n_tiles * pitch * LANES * 4 + 4 * lc * qcols * 4)
    kern = functools.partial(_ssm_kernel, lc=lc, pitch=pitch, nq=nq, tiles_per_q=tiles_per_q,
                             n_vregs=n_vregs)
    const2 = lambda b, c: (0, 0)
    const3 = lambda b, c: (0, 0, 0)
    return pl.pallas_call(
        kern,
        grid=(bsz, s // lc),
        in_specs=[pl.BlockSpec((1, lc, w_ssm), lambda b, c: (b, c, 0)),
                  pl.BlockSpec(bmat.shape, const3),
                  pl.BlockSpec(cmat.shape, const3),
                  pl.BlockSpec(ar.shape, const2),
                  pl.BlockSpec(ai.shape, const2),
                  pl.BlockSpec((1, w_ssm), const2)],
        out_specs=pl.BlockSpec((1, lc, w_ssm), lambda b, c: (b, c, 0)),
        out_shape=jax.ShapeDtypeStruct((bsz, s, w_ssm), BF16),
        scratch_shapes=[pltpu.VMEM((n_tiles * pitch, LANES), F32),
                        pltpu.VMEM((n_tiles * pitch, LANES), F32),
                        pltpu.VMEM((2, n_tiles, LANES), F32)],
        compiler_params=pltpu.CompilerParams(
            dimension_semantics=("parallel", "arbitrary"),
            vmem_limit_bytes=_vmem_limit(est)),
        name="s5_scan",
    )(proj, bmat, cmat, ar, ai, d.reshape(1, w_ssm).astype(F32))


def _attn_tiles(s, tq, tk):
    tq, tk = min(tq, s), min(tk, s)
    assert s % tq == 0 and tq % (2 * tk) == 0
    return tq, tk


def _key_query_iota(tk, tq):
    key = lax.broadcasted_iota(jnp.int32, (tk, tq), 0)
    query = lax.broadcasted_iota(jnp.int32, (tk, tq), 1)
    return key, query


def _transpose_values(v_ref, vt_ref, hs, chunk=512):
    s = v_ref.shape[1]
    chunk = min(chunk, s)

    @pl.when(pl.program_id(2) == 0)
    def _():
        for h in range(hs):
            for c in range(s // chunk):
                rows = slice(c * chunk, (c + 1) * chunk)
                vt_ref[h, :, rows] = v_ref[0, rows, h * HEAD_DIM:(h + 1) * HEAD_DIM].T


def _sb_kernel(q_ref, k_ref, v_ref, o_ref, z_ref, hi_ref, lo_ref, acc_ref, vt_ref, carry_ref, *, tq, tk, hs):
    qi = pl.program_id(2)
    r = tq // tk
    heads = range(hs)
    kk = lax.broadcasted_iota(jnp.int32, (tk, tk), 0)
    kk2 = lax.broadcasted_iota(jnp.int32, (tk, tk), 1)
    at_or_after = (kk2 >= kk).astype(BF16)

    def head_cols(h):
        return slice(h * HEAD_DIM, (h + 1) * HEAD_DIM)

    q = [q_ref[0, :, head_cols(h)] for h in heads]
    _transpose_values(v_ref, vt_ref, hs)

    def keep_logs(h, slot, kb, q0=0, mask=None):
        k = k_ref[0, pl.ds(pl.multiple_of(kb * tk, tk), tk), head_cols(h)]
        z = lax.dot_general(k, q[h][q0:, :], _NT, preferred_element_type=F32)
        nk = jnp.maximum(z, jnp.log2(1.0 + jnp.exp2(jnp.minimum(z, EXP2_CLAMP))))
        if mask is not None:
            nk = jnp.where(mask, nk, 0.0)
        hi = nk.astype(BF16)
        z_ref[h, slot, :, q0:] = z
        hi_ref[h, slot, :, q0:] = hi
        lo_ref[h, slot, :, q0:] = (nk - hi.astype(F32)).astype(BF16)

    def accumulate(h, slot, kb, carry, q0=0, mask=None):
        vt = vt_ref[h, :, pl.ds(pl.multiple_of(kb * tk, tk), tk)]
        incl = (jnp.dot(at_or_after, hi_ref[h, slot, :, q0:], preferred_element_type=F32)
                + jnp.dot(at_or_after, lo_ref[h, slot, :, q0:], preferred_element_type=F32))
        w = jnp.exp2(z_ref[h, slot, :, q0:] - incl - carry)
        if mask is not None:
            w = jnp.where(mask, w, 0.0)
        acc_ref[h, :, q0:] += jnp.dot(vt, w.astype(BF16), preferred_element_type=F32)
        return carry + incl[0:1, :]

    carry_ref[...] = jnp.zeros_like(carry_ref)
    acc_ref[...] = jnp.zeros_like(acc_ref)
    for d in reversed(range(r)):
        q0 = d * tk
        key, query = _key_query_iota(tk, tq - q0)
        mask = key < query
        for h in heads:
            keep_logs(h, d % 2, qi * r + d, q0, mask)
        for h in heads:
            carry_ref[h, :, q0:] = accumulate(h, d % 2, qi * r + d, carry_ref[h, :, q0:], q0, mask)
    carry = [carry_ref[h] for h in heads]

    n_full = qi * r
    for h in heads:
        keep_logs(h, 0, jnp.maximum(n_full - 1, 0))

    def body(it, carry):
        kb = n_full - 1 - 2 * it
        for h in heads:
            keep_logs(h, 1, kb - 1)
        carry = [accumulate(h, 0, kb, carry[h]) for h in heads]
        for h in heads:
            keep_logs(h, 0, jnp.maximum(kb - 2, 0))
        return [accumulate(h, 1, kb - 1, carry[h]) for h in heads]

    lax.fori_loop(0, n_full // 2, body, carry)
    for h in heads:
        o_ref[0, :, head_cols(h)] = acc_ref[h].T.astype(o_ref.dtype)


def _sb_attention(proj, *, heads, col0, tq, tk):
    bsz, s, _ = proj.shape
    tq, tk = _attn_tiles(s, tq, tk)
    hs = min(SB_HEADS_PER_STEP, heads)
    wcols = hs * HEAD_DIM
    assert heads % hs == 0 and col0 % wcols == 0 and (heads * HEAD_DIM) % wcols == 0
    c0, hg = col0 // wcols, heads // hs
    est = 2 * (2 * s * wcols * 2) + 4 * tq * wcols * 2 + hs * 16 * tq * tk * 4
    return pl.pallas_call(
        functools.partial(_sb_kernel, tq=tq, tk=tk, hs=hs),
        grid=(bsz, hg, s // tq),
        in_specs=[pl.BlockSpec((1, tq, wcols), lambda b, h, i: (b, i, c0 + h)),
                  pl.BlockSpec((1, s, wcols), lambda b, h, i: (b, 0, c0 + hg + h)),
                  pl.BlockSpec((1, s, wcols), lambda b, h, i: (b, 0, c0 + 2 * hg + h))],
        out_specs=pl.BlockSpec((1, tq, wcols), lambda b, h, i: (b, i, h)),
        out_shape=jax.ShapeDtypeStruct((bsz, s, heads * HEAD_DIM), BF16),
        scratch_shapes=[pltpu.VMEM((hs, 2, tk, tq), F32), pltpu.VMEM((hs, 2, tk, tq), BF16),
                        pltpu.VMEM((hs, 2, tk, tq), BF16), pltpu.VMEM((hs, HEAD_DIM, tq), F32),
                        pltpu.VMEM((hs, HEAD_DIM, s), BF16), pltpu.VMEM((hs, 1, tq), F32)],
        compiler_params=pltpu.CompilerParams(
            dimension_semantics=("parallel", "parallel", "arbitrary"),
            vmem_limit_bytes=_vmem_limit(est)),
        name="stickbreak_attn",
    )(proj, proj, proj)


def _cum_kernel(f_ref, b_ref, o_ref, kc_ref, carry_ref, *, chunk, heads):
    ts, w = f_ref.shape[1], f_ref.shape[2]

    @pl.when(pl.program_id(1) == 0)
    def _():
        carry_ref[...] = jnp.zeros_like(carry_ref)

    row = lax.broadcasted_iota(jnp.int32, (chunk, chunk), 0)
    col = lax.broadcasted_iota(jnp.int32, (chunk, chunk), 1)
    tri = (col <= row).astype(F32)
    src = lax.broadcasted_iota(jnp.int32, (CUM_PIECES * w, heads * HEAD_DIM), 0)
    dst = lax.broadcasted_iota(jnp.int32, (CUM_PIECES * w, heads * HEAD_DIM), 1)
    place = ((src % w) * HEAD_DIM + src // w == dst).astype(BF16)
    carry = carry_ref[...]
    for c in range(ts // chunk):
        rows = slice(c * chunk, (c + 1) * chunk)
        ls = _log_sigmoid(f_ref[0, rows, :] + b_ref[...])
        cs = jnp.dot(tri, ls, preferred_element_type=F32, precision=lax.Precision.HIGHEST) + carry
        carry = cs[chunk - 1:chunk, :]
        o_ref[0, rows, :] = cs * LOG2E
    carry_ref[...] = carry
    rest = -o_ref[0]
    pieces = []
    for _ in range(CUM_PIECES):
        piece = rest.astype(BF16)
        pieces.append(piece)
        rest = rest - piece.astype(F32)
    kc_ref[0] = jnp.dot(jnp.concatenate(pieces, axis=1), place, preferred_element_type=F32).astype(BF16)


def _forget_cumsum(f_logit, b_f, *, heads, chunk=256, ts=512):
    bsz, s, w = f_logit.shape
    ts = min(ts, s)
    chunk = min(chunk, ts)
    assert s % ts == 0 and ts % chunk == 0 and heads <= w
    return pl.pallas_call(
        functools.partial(_cum_kernel, chunk=chunk, heads=heads),
        grid=(bsz, s // ts),
        in_specs=[pl.BlockSpec((1, ts, w), lambda b, t: (b, t, 0)), pl.BlockSpec((1, w), lambda b, t: (0, 0))],
        out_specs=[pl.BlockSpec((1, ts, w), lambda b, t: (b, t, 0)),
                   pl.BlockSpec((1, ts, heads * HEAD_DIM), lambda b, t: (b, t, 0))],
        out_shape=[jax.ShapeDtypeStruct((bsz, s, w), F32),
                   jax.ShapeDtypeStruct((bsz, s, heads * HEAD_DIM), BF16)],
        scratch_shapes=[pltpu.VMEM((1, w), F32)],
        compiler_params=pltpu.CompilerParams(dimension_semantics=("parallel", "arbitrary")),
        name="forget_cumsum",
    )(f_logit, b_f)


def _fox_kernel(q_ref, k_ref, kc_ref, v_ref, ci_ref, o_ref, z_ref, p_ref, acc_ref, vt_ref, st_ref,
                *, tq, tk, hs):
    qi = pl.program_id(2)
    r = tq // tk
    heads = range(hs)
    lane = lax.broadcasted_iota(jnp.int32, (tq, HEAD_DIM), 1)
    ones = (lane < CUM_PIECES).astype(BF16)

    def head_cols(h):
        return slice(h * HEAD_DIM, (h + 1) * HEAD_DIM)

    q = [jnp.concatenate([q_ref[0, :, head_cols(h)], ones], axis=1) for h in heads]
    _transpose_values(v_ref, vt_ref, hs)

    def logits(h, kb, q0=0):
        rows = pl.ds(pl.multiple_of(kb * tk, tk), tk)
        k = jnp.concatenate([k_ref[0, rows, head_cols(h)], kc_ref[0, rows, head_cols(h)]], axis=1)
        return lax.dot_general(k, q[h][q0:, :], _NT, preferred_element_type=F32)

    def softmax(h, z, st, q0=0):
        _, m, l = st
        cq = ci_ref[0, h, pl.ds(qi, 1), q0:]
        m_new = jnp.maximum(m, jnp.max(z, axis=0, keepdims=True) + cq)
        p = jnp.exp2(z - (m_new - cq))
        a = jnp.exp2(m - m_new)
        return p.astype(BF16), (a, m_new, a * l + jnp.sum(p, axis=0, keepdims=True))

    def rescale_add(h, st, p, kb):
        vt = vt_ref[h, :, pl.ds(pl.multiple_of(kb * tk, tk), tk)]
        return st[0] * acc_ref[h] + jnp.dot(vt, p, preferred_element_type=F32)

    st_ref[:, 0] = jnp.ones_like(st_ref[:, 0])
    st_ref[:, 1] = jnp.full_like(st_ref[:, 1], -jnp.inf)
    st_ref[:, 2] = jnp.zeros_like(st_ref[:, 2])
    acc_ref[...] = jnp.zeros_like(acc_ref)
    for d in range(r):
        q0 = d * tk
        key, query = _key_query_iota(tk, tq - q0)
        mask = key <= query
        z = [jnp.where(mask, logits(h, qi * r + d, q0), -jnp.inf) for h in heads]
        if d > 0:
            for h in heads:
                acc_ref[h] = rescale_add(h, (st_ref[h, 0],), p_ref[h, (d - 1) % 2], qi * r + d - 1)
        for h in heads:
            if q0:
                p_ref[h, d % 2, :, :q0] = jnp.zeros((tk, q0), BF16)
                st_ref[h, 0, :, :q0] = jnp.ones((1, q0), F32)
            p, (a, m, l) = softmax(h, z[h], (None, st_ref[h, 1, :, q0:], st_ref[h, 2, :, q0:]), q0)
            p_ref[h, d % 2, :, q0:] = p
            st_ref[h, 0, :, q0:] = a
            st_ref[h, 1, :, q0:] = m
            st_ref[h, 2, :, q0:] = l
    st = [(st_ref[h, 0], st_ref[h, 1], st_ref[h, 2]) for h in heads]

    n_full = qi * r
    for h in heads:
        z_ref[h, 0] = logits(h, jnp.maximum(n_full - 1, 0))

    def step(slot, kb, kb_p, st):
        st = list(st)
        for h in heads:
            z_ref[h, 1 - slot] = logits(h, jnp.maximum(kb - 1, 0))
        for h in heads:
            acc_ref[h] = rescale_add(h, st[h], p_ref[h, 1 - slot], kb_p)
        for h in heads:
            p_ref[h, slot], st[h] = softmax(h, z_ref[h, slot], st[h])
        return st

    def body(it, st):
        kb = n_full - 1 - 2 * it
        st = step(0, kb, jnp.where(it == 0, qi * r + r - 1, kb + 1), st)
        return step(1, kb - 1, kb, st)

    st = lax.fori_loop(0, n_full // 2, body, st)
    for h in heads:
        acc = rescale_add(h, st[h], p_ref[h, 1], jnp.where(qi == 0, r - 1, 0))
        o_ref[0, :, head_cols(h)] = (acc / st[h][2]).T.astype(o_ref.dtype)


def _fox_attention(qkv, kc, cum_t, *, heads, tq, tk):
    bsz, s, _ = qkv.shape
    tq, tk = _attn_tiles(s, tq, tk)
    nq = s // tq
    ci = cum_t.reshape(bsz, heads, nq, tq)
    hs = min(FOX_HEADS_PER_STEP, heads)
    wcols = hs * HEAD_DIM
    assert heads % hs == 0
    hg = heads // hs
    est = 2 * (3 * s * wcols * 2) + 4 * tq * wcols * 2 + hs * 16 * tq * tk * 4
    return pl.pallas_call(
        functools.partial(_fox_kernel, tq=tq, tk=tk, hs=hs),
        grid=(bsz, hg, nq),
        in_specs=[pl.BlockSpec((1, tq, wcols), lambda b, h, i: (b, i, h)),
                  pl.BlockSpec((1, s, wcols), lambda b, h, i: (b, 0, hg + h)),
                  pl.BlockSpec((1, s, wcols), lambda b, h, i: (b, 0, h)),
                  pl.BlockSpec((1, s, wcols), lambda b, h, i: (b, 0, 2 * hg + h)),
                  pl.BlockSpec((1, hs, nq, tq), lambda b, h, i: (b, h, 0, 0))],
        out_specs=pl.BlockSpec((1, tq, wcols), lambda b, h, i: (b, i, h)),
        out_shape=jax.ShapeDtypeStruct((bsz, s, heads * HEAD_DIM), BF16),
        scratch_shapes=[pltpu.VMEM((hs, 2, tk, tq), F32), pltpu.VMEM((hs, 2, tk, tq), BF16),
                        pltpu.VMEM((hs, HEAD_DIM, tq), F32), pltpu.VMEM((hs, HEAD_DIM, s), BF16),
                        pltpu.VMEM((hs, 3, 1, tq), F32)],
        compiler_params=pltpu.CompilerParams(
            dimension_semantics=("parallel", "parallel", "arbitrary"),
            vmem_limit_bytes=_vmem_limit(est)),
        name="fox_attn",
    )(qkv, qkv, kc, qkv, ci)


MM_TM, MM_TN = 1024, 512
KT_TM, KT_TN, KT_TK = 1024, 1024, 2048
LN_ROWS = 256
SSM_CHUNK = 256
SB_TQ, SB_TK, SB_HEADS_PER_STEP = 1024, 256, 4
FOX_TQ, FOX_TK, FOX_HEADS_PER_STEP = 1024, 256, 4


def _mlp_and_norms(h32, hb, mix_fn, ln_mix_g, ln_mix_b, w1, w2, ln_ffn_g, ln_ffn_b, *, layer, alpha):
    z = mix_fn(h32, hb)
    h32, hb = _layer_norm(z, ln_mix_g[layer], ln_mix_b[layer], tr=LN_ROWS, name=f"ln_mix_{layer}")
    hid = _matmul_w32([hb], w1, layer, tm=MM_TM, tn=MM_TN, epilogue="relu2", name=f"mlp_up_{layer}")
    z = _matmul_ktiled(hid, w2, layer, tm=KT_TM, tn=KT_TN, tk=KT_TK, epilogue="resid", extra=h32,
                       alpha=alpha, out_dtype=F32, name=f"mlp_down_{layer}")
    return _layer_norm(z, ln_ffn_g[layer], ln_ffn_b[layer], tr=LN_ROWS, name=f"ln_ffn_{layer}")


def kernel(x, even_w_in, ssm_a_re, ssm_a_im, ssm_log_dt, ssm_b_re, ssm_b_im, ssm_c_re, ssm_c_im, ssm_d,
           ssm_w_glu, even_w_out, fox_w_in, fox_b_f, fox_w_out, ln_mix_g, ln_mix_b, mlp_w1, mlp_w2,
           ln_ffn_g, ln_ffn_b):
    bsz, s, d_model = x.shape
    depth = ln_mix_g.shape[0]
    alpha = (2.0 * depth) ** 0.25
    m = bsz * s
    w_ssm = ssm_d.shape[-1]

    h32 = x.reshape(m, d_model)
    hb = h32.astype(BF16)
    mlp_w2_b = mlp_w2.astype(BF16)
    for layer in range(depth):
        i = layer // 2
        if layer % 2 == 0:
            def mix_fn(h32, hb, i=i):
                n_in = even_w_in.shape[-1]
                sb_heads = (n_in - w_ssm) // (3 * HEAD_DIM)
                proj = _matmul_w32([hb], even_w_in, i, tm=MM_TM, tn=MM_TN, epilogue="colscale",
                                   extra=_query_colscale(n_in, w_ssm, sb_heads * HEAD_DIM), name="even_in")
                proj3 = proj.reshape(bsz, s, -1)
                bmat, cmat, ar, ai = _ssm_discretise(ssm_a_re[i], ssm_a_im[i], ssm_log_dt[i], ssm_b_re[i],
                                                     ssm_b_im[i], ssm_c_re[i], ssm_c_im[i])
                g = _ssm_scan(proj3, bmat, cmat, ar, ai, ssm_d[i], w_ssm=w_ssm, lc=SSM_CHUNK).reshape(m, w_ssm)
                y_ssm = _matmul_w32([g], ssm_w_glu, i, tm=MM_TM, tn=MM_TN, epilogue="glu", extra=g,
                                    name="ssm_glu")
                y_sb = _sb_attention(proj3, heads=sb_heads, col0=w_ssm, tq=SB_TQ, tk=SB_TK).reshape(m, -1)
                return _matmul_w32([y_ssm, y_sb], even_w_out, i, tm=MM_TM, tn=MM_TN, epilogue="resid",
                                   extra=h32, alpha=alpha, out_dtype=F32, name="even_out")
        else:
            def mix_fn(h32, hb, i=i):
                heads = fox_b_f.shape[-1]
                width = heads * HEAD_DIM
                w_nk = jnp.swapaxes(fox_w_in, 1, 2)
                tn = min(MM_TN, 3 * width)
                qkv = _matmul_w32([hb], w_nk, i, tm=MM_TM, tn=tn, col_blocks=3 * width // tn, b_is_nk=True,
                                  epilogue="colscale", extra=_query_colscale(3 * width, 0, width), name="fox_in")
                qkv = qkv.reshape(bsz, s, 3 * width)
                w_f = jnp.pad(w_nk[i:i + 1, 3 * width:, :], ((0, 0), (0, LANES - heads), (0, 0)))
                b_f = jnp.pad(fox_b_f[i].astype(F32), (0, LANES - heads)).reshape(1, LANES)
                f_logit = _matmul_w32([hb], w_f, 0, tm=MM_TM, tn=LANES, b_is_nk=True, out_dtype=F32,
                                      name="fox_forget")
                cum, kc = _forget_cumsum(f_logit.reshape(bsz, s, LANES), b_f, heads=heads)
                cum_t = cum[:, :, :heads].transpose(0, 2, 1)
                y = _fox_attention(qkv, kc, cum_t, heads=heads, tq=FOX_TQ, tk=FOX_TK)
                return _matmul_w32([y.reshape(m, width)], fox_w_out, i, tm=MM_TM, tn=MM_TN, epilogue="resid",
                                   extra=h32, alpha=alpha, out_dtype=F32, name="fox_out")
        h32, hb = _mlp_and_norms(h32, hb, mix_fn, ln_mix_g, ln_mix_b, mlp_w1, mlp_w2_b, ln_ffn_g, ln_ffn_b,
                                 layer=layer, alpha=alpha)
    return h32.reshape(bsz, s, d_model)
```

```python
import functools
import math

import jax
import jax.numpy as jnp
from jax import lax
from jax.experimental import pallas as pl
from jax.experimental.pallas import tpu as pltpu

F32 = jnp.float32
BF16 = jnp.bfloat16

HEAD_DIM = 128
LN_EPS = 1e-5
LOG2E = math.log2(math.e)
LANES = 128
SUBLANES = 8
VMEM_CAP = 56 * 1024 * 1024
CUM_PIECES = 3
EXP2_CLAMP = 126.0

_NT = (((1,), (1,)), ((), ()))


def _vmem_limit(estimate_bytes):
    return int(min(VMEM_CAP, max(32 * 1024 * 1024, estimate_bytes * 5 // 4)))


def _log_sigmoid(x):
    return jnp.minimum(x, 0.0) - jnp.log(1.0 + jnp.exp(-jnp.abs(x)))


def _gelu_tanh(x):
    c = math.sqrt(2.0 / math.pi)
    return 0.5 * x * (1.0 + jnp.tanh(c * (x + 0.044715 * (x * x * x))))


def _epilogue(acc, e_ref, o_ref, epilogue, alpha):
    if epilogue == "cast":
        o_ref[...] = acc.astype(o_ref.dtype)
    elif epilogue == "colscale":
        o_ref[...] = (acc * e_ref[...]).astype(o_ref.dtype)
    elif epilogue == "relu2":
        r = jnp.maximum(acc, 0.0)
        o_ref[...] = (r * r).astype(o_ref.dtype)
    elif epilogue == "glu":
        g = e_ref[...].astype(F32)
        o_ref[...] = (g * (1.0 / (1.0 + jnp.exp(-acc)))).astype(o_ref.dtype)
    elif epilogue == "resid":
        o_ref[...] = (alpha * e_ref[...].astype(F32) + acc).astype(o_ref.dtype)
    else:
        raise ValueError(epilogue)


def _mm_w32_kernel(*refs, n_a, epilogue, alpha, has_extra, b_is_nk):
    a_refs = refs[:n_a]
    b_ref = refs[n_a]
    e_ref = refs[n_a + 1] if has_extra else None
    o_ref, wb_ref = refs[-2], refs[-1]

    @pl.when(pl.program_id(1) == 0)
    def _():
        w = b_ref[...].T if b_is_nk else b_ref[...]
        wb_ref[...] = w.astype(BF16)

    acc = None
    k0 = 0
    for a_ref in a_refs:
        kw = a_ref.shape[1]
        part = jnp.dot(a_ref[...], wb_ref[k0:k0 + kw, :], preferred_element_type=F32)
        acc = part if acc is None else acc + part
        k0 += kw
    _epilogue(acc, e_ref, o_ref, epilogue, alpha)


def _mm_ktiled_kernel(a_ref, b_ref, e_ref, o_ref, acc_ref, *, nk, epilogue, alpha):
    k = pl.program_id(2)

    @pl.when(k == 0)
    def _():
        acc_ref[...] = jnp.zeros_like(acc_ref)

    acc_ref[...] += jnp.dot(a_ref[...], b_ref[...], preferred_element_type=F32)

    @pl.when(k == nk - 1)
    def _():
        _epilogue(acc_ref[...], e_ref, o_ref, epilogue, alpha)


def _matmul_ktiled(a, b, layer, *, tm, tn, tk, epilogue, extra, alpha, out_dtype, name):
    m, kdim = a.shape
    _, k2, n = b.shape
    assert kdim == k2
    tm, tn, tk = min(tm, m), min(tn, n), min(tk, kdim)
    assert m % tm == 0 and n % tn == 0 and kdim % tk == 0
    nk = kdim // tk
    est = 2 * (tm * tk + tk * tn) * 2 + 2 * tm * tn * (jnp.dtype(out_dtype).itemsize + extra.dtype.itemsize)
    est += 2 * tm * tn * 4
    return pl.pallas_call(
        functools.partial(_mm_ktiled_kernel, nk=nk, epilogue=epilogue, alpha=alpha),
        grid=(m // tm, n // tn, nk),
        in_specs=[pl.BlockSpec((tm, tk), lambda i, j, k: (i, k)),
                  pl.BlockSpec((None, tk, tn), lambda i, j, k: (layer, k, j)),
                  pl.BlockSpec((tm, tn), lambda i, j, k: (i, j))],
        out_specs=pl.BlockSpec((tm, tn), lambda i, j, k: (i, j)),
        out_shape=jax.ShapeDtypeStruct((m, n), out_dtype),
        scratch_shapes=[pltpu.VMEM((tm, tn), F32)],
        compiler_params=pltpu.CompilerParams(
            dimension_semantics=("parallel", "parallel", "arbitrary"),
            vmem_limit_bytes=_vmem_limit(est)),
        name=name,
    )(a, b, extra)


def _matmul_w32(a_parts, b, layer, *, tm, tn, col_blocks=None, b_is_nk=False, epilogue="cast", extra=None,
                alpha=1.0, out_dtype=BF16, name):
    a_parts = tuple(a_parts)
    m = a_parts[0].shape[0]
    kdim, n_all = (b.shape[2], b.shape[1]) if b_is_nk else (b.shape[1], b.shape[2])
    assert sum(a.shape[1] for a in a_parts) == kdim
    tm, tn = min(tm, m), min(tn, n_all)
    nj = n_all // tn if col_blocks is None else col_blocks
    n = nj * tn
    assert m % tm == 0 and n <= n_all
    has_extra = extra is not None
    in_specs = [pl.BlockSpec((tm, a.shape[1]), lambda j, i: (i, 0)) for a in a_parts]
    if b_is_nk:
        in_specs.append(pl.BlockSpec((None, tn, kdim), lambda j, i: (layer, j, 0)))
    else:
        in_specs.append(pl.BlockSpec((None, kdim, tn), lambda j, i: (layer, 0, j)))
    args = list(a_parts) + [b]
    est = 2 * tm * kdim * 2 + 2 * kdim * tn * 4 + kdim * tn * 2
    est += 2 * tm * tn * jnp.dtype(out_dtype).itemsize + 2 * tm * tn * 4
    if has_extra:
        if epilogue == "colscale":
            in_specs.append(pl.BlockSpec((1, tn), lambda j, i: (0, j)))
        else:
            in_specs.append(pl.BlockSpec((tm, tn), lambda j, i: (i, j)))
            est += 2 * tm * tn * extra.dtype.itemsize
        args.append(extra)
    return pl.pallas_call(
        functools.partial(_mm_w32_kernel, n_a=len(a_parts), epilogue=epilogue, alpha=alpha,
                          has_extra=has_extra, b_is_nk=b_is_nk),
        grid=(nj, m // tm),
        in_specs=in_specs,
        out_specs=pl.BlockSpec((tm, tn), lambda j, i: (i, j)),
        out_shape=jax.ShapeDtypeStruct((m, n), out_dtype),
        scratch_shapes=[pltpu.VMEM((kdim, tn), BF16)],
        compiler_params=pltpu.CompilerParams(
            dimension_semantics=("parallel", "arbitrary"),
            vmem_limit_bytes=_vmem_limit(est)),
        name=name,
    )(*args)


def _query_colscale(n_cols, q_start, q_width):
    col = jnp.arange(n_cols)
    is_q = (col >= q_start) & (col < q_start + q_width)
    return jnp.where(is_q, HEAD_DIM ** -0.5 * LOG2E, 1.0).astype(F32).reshape(1, n_cols)


def _ln_kernel(x_ref, g_ref, b_ref, of_ref, ob_ref):
    x = x_ref[...]
    mu = jnp.mean(x, axis=-1, keepdims=True)
    xc = x - mu
    var = jnp.mean(xc * xc, axis=-1, keepdims=True)
    y = xc * lax.rsqrt(var + LN_EPS) * g_ref[...] + b_ref[...]
    of_ref[...] = y
    ob_ref[...] = y.astype(BF16)


def _layer_norm(x, g, b, *, tr, name):
    m, d = x.shape
    tr = min(tr, m)
    assert m % tr == 0
    row = pl.BlockSpec((tr, d), lambda i: (i, 0))
    vec = pl.BlockSpec((1, d), lambda i: (0, 0))
    return pl.pallas_call(
        _ln_kernel,
        grid=(m // tr,),
        in_specs=[row, vec, vec],
        out_specs=[row, row],
        out_shape=[jax.ShapeDtypeStruct((m, d), F32), jax.ShapeDtypeStruct((m, d), BF16)],
        compiler_params=pltpu.CompilerParams(
            dimension_semantics=("parallel",),
            vmem_limit_bytes=_vmem_limit(2 * tr * d * (4 + 4 + 2) + 4 * tr * d * 4)),
        name=name,
    )(x, g.reshape(1, d).astype(F32), b.reshape(1, d).astype(F32))


def _ssm_kernel(u_ref, bm_ref, cm_ref, ar_ref, ai_ref, d_ref, o_ref, sre, sim, st_ref,
                *, lc, pitch, nq, tiles_per_q, n_vregs):
    c = pl.program_id(1)

    @pl.when(c == 0)
    def _():
        st_ref[...] = jnp.zeros_like(st_ref)

    for q in range(nq):
        uq = u_ref[0, :, q * LANES:(q + 1) * LANES]
        res = jnp.dot(uq, bm_ref[q], preferred_element_type=F32)
        for tl in range(tiles_per_q):
            base = (q * tiles_per_q + tl) * pitch
            sre[base:base + lc, :] = res[:, (2 * tl) * LANES:(2 * tl + 1) * LANES]
            sim[base:base + lc, :] = res[:, (2 * tl + 1) * LANES:(2 * tl + 2) * LANES]

    ar = [ar_ref[j * SUBLANES:(j + 1) * SUBLANES, :] for j in range(n_vregs)]
    ai = [ai_ref[j * SUBLANES:(j + 1) * SUBLANES, :] for j in range(n_vregs)]
    xr0 = tuple(st_ref[0, j * SUBLANES:(j + 1) * SUBLANES, :] for j in range(n_vregs))
    xi0 = tuple(st_ref[1, j * SUBLANES:(j + 1) * SUBLANES, :] for j in range(n_vregs))

    def step(t, carry):
        xr, xi = carry
        nxr, nxi = [], []
        for j in range(n_vregs):
            rows = pl.ds(j * SUBLANES * pitch + t, SUBLANES, stride=pitch)
            nr = ar[j] * xr[j] - ai[j] * xi[j] + sre[rows, :]
            ni = ar[j] * xi[j] + ai[j] * xr[j] + sim[rows, :]
            sre[rows, :] = nr
            sim[rows, :] = ni
            nxr.append(nr)
            nxi.append(ni)
        return tuple(nxr), tuple(nxi)

    xr, xi = lax.fori_loop(0, lc, step, (xr0, xi0))
    for j in range(n_vregs):
        st_ref[0, j * SUBLANES:(j + 1) * SUBLANES, :] = xr[j]
        st_ref[1, j * SUBLANES:(j + 1) * SUBLANES, :] = xi[j]

    for q in range(nq):
        pieces = []
        for tl in range(tiles_per_q):
            base = (q * tiles_per_q + tl) * pitch
            pieces.append(sre[base:base + lc, :].astype(BF16))
            pieces.append(sim[base:base + lc, :].astype(BF16))
        xcat = jnp.concatenate(pieces, axis=1)
        y = jnp.dot(xcat, cm_ref[q], preferred_element_type=F32)
        uq = u_ref[0, :, q * LANES:(q + 1) * LANES].astype(F32)
        y = y + d_ref[:, q * LANES:(q + 1) * LANES] * uq
        o_ref[0, :, q * LANES:(q + 1) * LANES] = _gelu_tanh(y).astype(o_ref.dtype)


def _ssm_discretise(a_re, a_im, log_dt, b_re, b_im, c_re, c_im):
    g, n = a_re.shape
    p = b_re.shape[-1]
    lr = a_re.astype(F32)
    li = a_im.astype(F32)
    dt = jnp.exp(log_dt.astype(F32))[:, None]
    mag = jnp.exp(lr * dt)
    abar_r = mag * jnp.cos(li * dt)
    abar_i = mag * jnp.sin(li * dt)
    den = lr * lr + li * li
    nr = abar_r - 1.0
    ni = abar_i
    zr = (nr * lr + ni * li) / den
    zi = (ni * lr - nr * li) / den
    br = b_re.astype(F32)
    bi = b_im.astype(F32)
    bbar_r = zr[..., None] * br - zi[..., None] * bi
    bbar_i = zr[..., None] * bi + zi[..., None] * br

    gq = LANES // p
    gt = LANES // n
    nq = g // gq
    tq = gq // gt
    eye = jnp.eye(gq, dtype=F32)
    bb = jnp.stack([bbar_r, bbar_i], 0).reshape(2, nq, gq, n, p)
    bf = jnp.einsum("aqhnp,gh->qgpahn", bb, eye)
    bf = bf.reshape(nq, gq, p, 2, tq, gt, n).transpose(0, 1, 2, 4, 3, 5, 6)
    bmat = bf.reshape(nq, gq * p, tq * 2 * gt * n).astype(BF16)
    cc = jnp.stack([c_re.astype(F32), -c_im.astype(F32)], 0).reshape(2, nq, gq, p, n)
    cf = jnp.einsum("aqhpn,gh->qahngp", cc, eye)
    cf = cf.reshape(nq, 2, tq, gt, n, gq, p).transpose(0, 2, 1, 3, 4, 5, 6)
    cmat = cf.reshape(nq, tq * 2 * gt * n, gq * p).astype(BF16)
    n_tiles = g * n // LANES
    return bmat, cmat, abar_r.reshape(n_tiles, LANES), abar_i.reshape(n_tiles, LANES)


def _ssm_scan(proj, bmat, cmat, ar, ai, d, *, w_ssm, lc):
    bsz, s, _ = proj.shape
    lc = min(lc, s)
    assert s % lc == 0 and lc % SUBLANES == 0
    nq, _, qcols = bmat.shape
    tiles_per_q = qcols // (2 * LANES)
    n_tiles = ar.shape[0]
    assert n_tiles == nq * tiles_per_q and n_tiles % SUBLANES == 0 and w_ssm == nq * LANES
    n_vregs = n_tiles // SUBLANES
    pitch = lc + 4
    est = (2 * 2 * lc * w_ssm * 2 + 2 * (bmat.size + cmat.size) * 2
           + 2 * n_tiles * pitch * LANES * 4 + 4 * lc * qcols * 4)
    kern = functools.partial(_ssm_kernel, lc=lc, pitch=pitch, nq=nq, tiles_per_q=tiles_per_q,
                             n_vregs=n_vregs)
    const2 = lambda b, c: (0, 0)
    const3 = lambda b, c: (0, 0, 0)
    return pl.pallas_call(
        kern,
        grid=(bsz, s // lc),
        in_specs=[pl.BlockSpec((1, lc, w_ssm), lambda b, c: (b, c, 0)),
                  pl.BlockSpec(bmat.shape, const3),
                  pl.BlockSpec(cmat.shape, const3),
                  pl.BlockSpec(ar.shape, const2),
                  pl.BlockSpec(ai.shape, const2),
                  pl.BlockSpec((1, w_ssm), const2)],
        out_specs=pl.BlockSpec((1, lc, w_ssm), lambda b, c: (b, c, 0)),
        out_shape=jax.ShapeDtypeStruct((bsz, s, w_ssm), BF16),
        scratch_shapes=[pltpu.VMEM((n_tiles * pitch, LANES), F32),
                        pltpu.VMEM((n_tiles * pitch, LANES), F32),
                        pltpu.VMEM((2, n_tiles, LANES), F32)],
        compiler_params=pltpu.CompilerParams(
            dimension_semantics=("parallel", "arbitrary"),
            vmem_limit_bytes=_vmem_limit(est)),
        name="s5_scan",
    )(proj, bmat, cmat, ar, ai, d.reshape(1, w_ssm).astype(F32))


def _attn_tiles(s, tq, tk):
    tq, tk = min(tq, s), min(tk, s)
    assert s % tq == 0 and tq % (2 * tk) == 0
    return tq, tk


def _key_query_iota(tk, tq):
    key = lax.broadcasted_iota(jnp.int32, (tk, tq), 0)
    query = lax.broadcasted_iota(jnp.int32, (tk, tq), 1)
    return key, query


def _transpose_values(v_ref, vt_ref, hs, chunk=512):
    s = v_ref.shape[1]
    chunk = min(chunk, s)

    @pl.when(pl.program_id(2) == 0)
    def _():
        for h in range(hs):
            for c in range(s // chunk):
                rows = slice(c * chunk, (c + 1) * chunk)
                vt_ref[h, :, rows] = v_ref[0, rows, h * HEAD_DIM:(h + 1) * HEAD_DIM].T


def _sb_kernel(q_ref, k_ref, v_ref, o_ref, z_ref, hi_ref, lo_ref, acc_ref, vt_ref, carry_ref, *, tq, tk, hs):
    qi = pl.program_id(2)
    r = tq // tk
    heads = range(hs)
    kk = lax.broadcasted_iota(jnp.int32, (tk, tk), 0)
    kk2 = lax.broadcasted_iota(jnp.int32, (tk, tk), 1)
    at_or_after = (kk2 >= kk).astype(BF16)

    def head_cols(h):
        return slice(h * HEAD_DIM, (h + 1) * HEAD_DIM)

    q = [q_ref[0, :, head_cols(h)] for h in heads]
    _transpose_values(v_ref, vt_ref, hs)

    def keep_logs(h, slot, kb, q0=0, mask=None):
        k = k_ref[0, pl.ds(pl.multiple_of(kb * tk, tk), tk), head_cols(h)]
        z = lax.dot_general(k, q[h][q0:, :], _NT, preferred_element_type=F32)
        nk = jnp.maximum(z, jnp.log2(1.0 + jnp.exp2(jnp.minimum(z, EXP2_CLAMP))))
        if mask is not None:
            nk = jnp.where(mask, nk, 0.0)
        hi = nk.astype(BF16)
        z_ref[h, slot, :, q0:] = z
        hi_ref[h, slot, :, q0:] = hi
        lo_ref[h, slot, :, q0:] = (nk - hi.astype(F32)).astype(BF16)

    def accumulate(h, slot, kb, carry, q0=0, mask=None):
        vt = vt_ref[h, :, pl.ds(pl.multiple_of(kb * tk, tk), tk)]
        incl = (jnp.dot(at_or_after, hi_ref[h, slot, :, q0:], preferred_element_type=F32)
                + jnp.dot(at_or_after, lo_ref[h, slot, :, q0:], preferred_element_type=F32))
        w = jnp.exp2(z_ref[h, slot, :, q0:] - incl - carry)
        if mask is not None:
            w = jnp.where(mask, w, 0.0)
        acc_ref[h, :, q0:] += jnp.dot(vt, w.astype(BF16), preferred_element_type=F32)
        return carry + incl[0:1, :]

    carry_ref[...] = jnp.zeros_like(carry_ref)
    acc_ref[...] = jnp.zeros_like(acc_ref)
    for d in reversed(range(r)):
        q0 = d * tk
        key, query = _key_query_iota(tk, tq - q0)
        mask = key < query
        for h in heads:
            keep_logs(h, d % 2, qi * r + d, q0, mask)
        for h in heads:
            carry_ref[h, :, q0:] = accumulate(h, d % 2, qi * r + d, carry_ref[h, :, q0:], q0, mask)
    carry = [carry_ref[h] for h in heads]

    n_full = qi * r
    for h in heads:
        keep_logs(h, 0, jnp.maximum(n_full - 1, 0))

    def body(it, carry):
        kb = n_full - 1 - 2 * it
        for h in heads:
            keep_logs(h, 1, kb - 1)
        carry = [accumulate(h, 0, kb, carry[h]) for h in heads]
        for h in heads:
            keep_logs(h, 0, jnp.maximum(kb - 2, 0))
        return [accumulate(h, 1, kb - 1, carry[h]) for h in heads]

    lax.fori_loop(0, n_full // 2, body, carry)
    for h in heads:
        o_ref[0, :, head_cols(h)] = acc_ref[h].T.astype(o_ref.dtype)


def _sb_attention(proj, *, heads, col0, tq, tk):
    bsz, s, _ = proj.shape
    tq, tk = _attn_tiles(s, tq, tk)
    hs = min(SB_HEADS_PER_STEP, heads)
    wcols = hs * HEAD_DIM
    assert heads % hs == 0 and col0 % wcols == 0 and (heads * HEAD_DIM) % wcols == 0
    c0, hg = col0 // wcols, heads // hs
    est = 2 * (2 * s * wcols * 2) + 4 * tq * wcols * 2 + hs * 16 * tq * tk * 4
    return pl.pallas_call(
        functools.partial(_sb_kernel, tq=tq, tk=tk, hs=hs),
        grid=(bsz, hg, s // tq),
        in_specs=[pl.BlockSpec((1, tq, wcols), lambda b, h, i: (b, i, c0 + h)),
                  pl.BlockSpec((1, s, wcols), lambda b, h, i: (b, 0, c0 + hg + h)),
                  pl.BlockSpec((1, s, wcols), lambda b, h, i: (b, 0, c0 + 2 * hg + h))],
        out_specs=pl.BlockSpec((1, tq, wcols), lambda b, h, i: (b, i, h)),
        out_shape=jax.ShapeDtypeStruct((bsz, s, heads * HEAD_DIM), BF16),
        scratch_shapes=[pltpu.VMEM((hs, 2, tk, tq), F32), pltpu.VMEM((hs, 2, tk, tq), BF16),
                        pltpu.VMEM((hs, 2, tk, tq), BF16), pltpu.VMEM((hs, HEAD_DIM, tq), F32),
                        pltpu.VMEM((hs, HEAD_DIM, s), BF16), pltpu.VMEM((hs, 1, tq), F32)],
        compiler_params=pltpu.CompilerParams(
            dimension_semantics=("parallel", "parallel", "arbitrary"),
            vmem_limit_bytes=_vmem_limit(est)),
        name="stickbreak_attn",
    )(proj, proj, proj)


def _cum_kernel(f_ref, b_ref, o_ref, kc_ref, carry_ref, *, chunk, heads):
    ts, w = f_ref.shape[1], f_ref.shape[2]

    @pl.when(pl.program_id(1) == 0)
    def _():
        carry_ref[...] = jnp.zeros_like(carry_ref)

    row = lax.broadcasted_iota(jnp.int32, (chunk, chunk), 0)
    col = lax.broadcasted_iota(jnp.int32, (chunk, chunk), 1)
    tri = (col <= row).astype(F32)
    src = lax.broadcasted_iota(jnp.int32, (CUM_PIECES * w, heads * HEAD_DIM), 0)
    dst = lax.broadcasted_iota(jnp.int32, (CUM_PIECES * w, heads * HEAD_DIM), 1)
    place = ((src % w) * HEAD_DIM + src // w == dst).astype(BF16)
    carry = carry_ref[...]
    for c in range(ts // chunk):
        rows = slice(c * chunk, (c + 1) * chunk)
        ls = _log_sigmoid(f_ref[0, rows, :] + b_ref[...])
        cs = jnp.dot(tri, ls, preferred_element_type=F32, precision=lax.Precision.HIGHEST) + carry
        carry = cs[chunk - 1:chunk, :]
        o_ref[0, rows, :] = cs * LOG2E
    carry_ref[...] = carry
    rest = -o_ref[0]
    pieces = []
    for _ in range(CUM_PIECES):
        piece = rest.astype(BF16)
        pieces.append(piece)
        rest = rest - piece.astype(F32)
    kc_ref[0] = jnp.dot(jnp.concatenate(pieces, axis=1), place, preferred_element_type=F32).astype(BF16)


def _forget_cumsum(f_logit, b_f, *, heads, chunk=256, ts=512):
    bsz, s, w = f_logit.shape
    ts = min(ts, s)
    chunk = min(chunk, ts)
    assert s % ts == 0 and ts % chunk == 0 and heads <= w
    return pl.pallas_call(
        functools.partial(_cum_kernel, chunk=chunk, heads=heads),
        grid=(bsz, s // ts),
        in_specs=[pl.BlockSpec((1, ts, w), lambda b, t: (b, t, 0)), pl.BlockSpec((1, w), lambda b, t: (0, 0))],
        out_specs=[pl.BlockSpec((1, ts, w), lambda b, t: (b, t, 0)),
                   pl.BlockSpec((1, ts, heads * HEAD_DIM), lambda b, t: (b, t, 0))],
        out_shape=[jax.ShapeDtypeStruct((bsz, s, w), F32),
                   jax.ShapeDtypeStruct((bsz, s, heads * HEAD_DIM), BF16)],
        scratch_shapes=[pltpu.VMEM((1, w), F32)],
        compiler_params=pltpu.CompilerParams(dimension_semantics=("parallel", "arbitrary")),
        name="forget_cumsum",
    )(f_logit, b_f)


def _fox_kernel(q_ref, k_ref, kc_ref, v_ref, ci_ref, o_ref, z_ref, p_ref, acc_ref, vt_ref, st_ref,
                *, tq, tk, hs):
    qi = pl.program_id(2)
    r = tq // tk
    heads = range(hs)
    lane = lax.broadcasted_iota(jnp.int32, (tq, HEAD_DIM), 1)
    ones = (lane < CUM_PIECES).astype(BF16)

    def head_cols(h):
        return slice(h * HEAD_DIM, (h + 1) * HEAD_DIM)

    q = [jnp.concatenate([q_ref[0, :, head_cols(h)], ones], axis=1) for h in heads]
    _transpose_values(v_ref, vt_ref, hs)

    def logits(h, kb, q0=0):
        rows = pl.ds(pl.multiple_of(kb * tk, tk), tk)
        k = jnp.concatenate([k_ref[0, rows, head_cols(h)], kc_ref[0, rows, head_cols(h)]], axis=1)
        return lax.dot_general(k, q[h][q0:, :], _NT, preferred_element_type=F32)

    def softmax(h, z, st, q0=0):
        _, m, l = st
        cq = ci_ref[0, h, pl.ds(qi, 1), q0:]
        m_new = jnp.maximum(m, jnp.max(z, axis=0, keepdims=True) + cq)
        p = jnp.exp2(z - (m_new - cq))
        a = jnp.exp2(m - m_new)
        return p.astype(BF16), (a, m_new, a * l + jnp.sum(p, axis=0, keepdims=True))

    def rescale_add(h, st, p, kb):
        vt = vt_ref[h, :, pl.ds(pl.multiple_of(kb * tk, tk), tk)]
        return st[0] * acc_ref[h] + jnp.dot(vt, p, preferred_element_type=F32)

    st_ref[:, 0] = jnp.ones_like(st_ref[:, 0])
    st_ref[:, 1] = jnp.full_like(st_ref[:, 1], -jnp.inf)
    st_ref[:, 2] = jnp.zeros_like(st_ref[:, 2])
    acc_ref[...] = jnp.zeros_like(acc_ref)
    for d in range(r):
        q0 = d * tk
        key, query = _key_query_iota(tk, tq - q0)
        mask = key <= query
        z = [jnp.where(mask, logits(h, qi * r + d, q0), -jnp.inf) for h in heads]
        if d > 0:
            for h in heads:
                acc_ref[h] = rescale_add(h, (st_ref[h, 0],), p_ref[h, (d - 1) % 2], qi * r + d - 1)
        for h in heads:
            if q0:
                p_ref[h, d % 2, :, :q0] = jnp.zeros((tk, q0), BF16)
                st_ref[h, 0, :, :q0] = jnp.ones((1, q0), F32)
            p, (a, m, l) = softmax(h, z[h], (None, st_ref[h, 1, :, q0:], st_ref[h, 2, :, q0:]), q0)
            p_ref[h, d % 2, :, q0:] = p
            st_ref[h, 0, :, q0:] = a
            st_ref[h, 1, :, q0:] = m
            st_ref[h, 2, :, q0:] = l
    st = [(st_ref[h, 0], st_ref[h, 1], st_ref[h, 2]) for h in heads]

    n_full = qi * r
    for h in heads:
        z_ref[h, 0] = logits(h, jnp.maximum(n_full - 1, 0))

    def step(slot, kb, kb_p, st):
        st = list(st)
        for h in heads:
            z_ref[h, 1 - slot] = logits(h, jnp.maximum(kb - 1, 0))
        for h in heads:
            acc_ref[h] = rescale_add(h, st[h], p_ref[h, 1 - slot], kb_p)
        for h in heads:
            p_ref[h, slot], st[h] = softmax(h, z_ref[h, slot], st[h])
        return st

    def body(it, st):
        kb = n_full - 1 - 2 * it
        st = step(0, kb, jnp.where(it == 0, qi * r + r - 1, kb + 1), st)
        return step(1, kb - 1, kb, st)

    st = lax.fori_loop(0, n_full // 2, body, st)
    for h in heads:
        acc = rescale_add(h, st[h], p_ref[h, 1], jnp.where(qi == 0, r - 1, 0))
        o_ref[0, :, head_cols(h)] = (acc / st[h][2]).T.astype(o_ref.dtype)


def _fox_attention(qkv, kc, cum_t, *, heads, tq, tk):
    bsz, s, _ = qkv.shape
    tq, tk = _attn_tiles(s, tq, tk)
    nq = s // tq
    ci = cum_t.reshape(bsz, heads, nq, tq)
    hs = min(FOX_HEADS_PER_STEP, heads)
    wcols = hs * HEAD_DIM
    assert heads % hs == 0
    hg = heads // hs
    est = 2 * (3 * s * wcols * 2) + 4 * tq * wcols * 2 + hs * 16 * tq * tk * 4
    return pl.pallas_call(
        functools.partial(_fox_kernel, tq=tq, tk=tk, hs=hs),
        grid=(bsz, hg, nq),
        in_specs=[pl.BlockSpec((1, tq, wcols), lambda b, h, i: (b, i, h)),
                  pl.BlockSpec((1, s, wcols), lambda b, h, i: (b, 0, hg + h)),
                  pl.BlockSpec((1, s, wcols), lambda b, h, i: (b, 0, h)),
                  pl.BlockSpec((1, s, wcols), lambda b, h, i: (b, 0, 2 * hg + h)),
                  pl.BlockSpec((1, hs, nq, tq), lambda b, h, i: (b, h, 0, 0))],
        out_specs=pl.BlockSpec((1, tq, wcols), lambda b, h, i: (b, i, h)),
        out_shape=jax.ShapeDtypeStruct((bsz, s, heads * HEAD_DIM), BF16),
        scratch_shapes=[pltpu.VMEM((hs, 2, tk, tq), F32), pltpu.VMEM((hs, 2, tk, tq), BF16),
                        pltpu.VMEM((hs, HEAD_DIM, tq), F32), pltpu.VMEM((hs, HEAD_DIM, s), BF16),
                        pltpu.VMEM((hs, 3, 1, tq), F32)],
        compiler_params=pltpu.CompilerParams(
            dimension_semantics=("parallel", "parallel", "arbitrary"),
            vmem_limit_bytes=_vmem_limit(est)),
        name="fox_attn",
    )(qkv, qkv, kc, qkv, ci)


MM_TM, MM_TN = 1024, 512
KT_TM, KT_TN, KT_TK = 1024, 1024, 2048
LN_ROWS = 256
SSM_CHUNK = 256
SB_TQ, SB_TK, SB_HEADS_PER_STEP = 1024, 256, 4
FOX_TQ, FOX_TK, FOX_HEADS_PER_STEP = 512, 256, 4


def _mlp_and_norms(h32, hb, mix_fn, ln_mix_g, ln_mix_b, w1, w2, ln_ffn_g, ln_ffn_b, *, layer, alpha):
    z = mix_fn(h32, hb)
    h32, hb = _layer_norm(z, ln_mix_g[layer], ln_mix_b[layer], tr=LN_ROWS, name=f"ln_mix_{layer}")
    hid = _matmul_w32([hb], w1, layer, tm=MM_TM, tn=MM_TN, epilogue="relu2", name=f"mlp_up_{layer}")
    z = _matmul_ktiled(hid, w2, layer, tm=KT_TM, tn=KT_TN, tk=KT_TK, epilogue="resid", extra=h32,
                       alpha=alpha, out_dtype=F32, name=f"mlp_down_{layer}")
    return _layer_norm(z, ln_ffn_g[layer], ln_ffn_b[layer], tr=LN_ROWS, name=f"ln_ffn_{layer}")


def kernel(x, even_w_in, ssm_a_re, ssm_a_im, ssm_log_dt, ssm_b_re, ssm_b_im, ssm_c_re, ssm_c_im, ssm_d,
           ssm_w_glu, even_w_out, fox_w_in, fox_b_f, fox_w_out, ln_mix_g, ln_mix_b, mlp_w1, mlp_w2,
           ln_ffn_g, ln_ffn_b):
    bsz, s, d_model = x.shape
    depth = ln_mix_g.shape[0]
    alpha = (2.0 * depth) ** 0.25
    m = bsz * s
    w_ssm = ssm_d.shape[-1]

    h32 = x.reshape(m, d_model)
    hb = h32.astype(BF16)
    mlp_w2_b = mlp_w2.astype(BF16)
    for layer in range(depth):
        i = layer // 2
        if layer % 2 == 0:
            def mix_fn(h32, hb, i=i):
                n_in = even_w_in.shape[-1]
                sb_heads = (n_in - w_ssm) // (3 * HEAD_DIM)
                proj = _matmul_w32([hb], even_w_in, i, tm=MM_TM, tn=MM_TN, epilogue="colscale",
                                   extra=_query_colscale(n_in, w_ssm, sb_heads * HEAD_DIM), name="even_in")
                proj3 = proj.reshape(bsz, s, -1)
                bmat, cmat, ar, ai = _ssm_discretise(ssm_a_re[i], ssm_a_im[i], ssm_log_dt[i], ssm_b_re[i],
                                                     ssm_b_im[i], ssm_c_re[i], ssm_c_im[i])
                g = _ssm_scan(proj3, bmat, cmat, ar, ai, ssm_d[i], w_ssm=w_ssm, lc=SSM_CHUNK).reshape(m, w_ssm)
                y_ssm = _matmul_w32([g], ssm_w_glu, i, tm=MM_TM, tn=MM_TN, epilogue="glu", extra=g,
                                    name="ssm_glu")
                y_sb = _sb_attention(proj3, heads=sb_heads, col0=w_ssm, tq=SB_TQ, tk=SB_TK).reshape(m, -1)
                return _matmul_w32([y_ssm, y_sb], even_w_out, i, tm=MM_TM, tn=MM_TN, epilogue="resid",
                                   extra=h32, alpha=alpha, out_dtype=F32, name="even_out")
        else:
            def mix_fn(h32, hb, i=i):
                heads = fox_b_f.shape[-1]
                width = heads * HEAD_DIM
                w_nk = jnp.swapaxes(fox_w_in, 1, 2)
                tn = min(MM_TN, 3 * width)
                qkv = _matmul_w32([hb], w_nk, i, tm=MM_TM, tn=tn, col_blocks=3 * width // tn, b_is_nk=True,
                                  epilogue="colscale", extra=_query_colscale(3 * width, 0, width), name="fox_in")
                qkv = qkv.reshape(bsz, s, 3 * width)
                w_f = jnp.pad(w_nk[i:i + 1, 3 * width:, :], ((0, 0), (0, LANES - heads), (0, 0)))
                b_f = jnp.pad(fox_b_f[i].astype(F32), (0, LANES - heads)).reshape(1, LANES)
                f_logit = _matmul_w32([hb], w_f, 0, tm=MM_TM, tn=LANES, b_is_nk=True, out_dtype=F32,
                                      name="fox_forget")
                cum, kc = _forget_cumsum(f_logit.reshape(bsz, s, LANES), b_f, heads=heads)
                cum_t = cum[:, :, :heads].transpose(0, 2, 1)
                y = _fox_attention(qkv, kc, cum_t, heads=heads, tq=FOX_TQ, tk=FOX_TK)
                return _matmul_w32([y.reshape(m, width)], fox_w_out, i, tm=MM_TM, tn=MM_TN, epilogue="resid",
                                   extra=h32, alpha=alpha, out_dtype=F32, name="fox_out")
        h32, hb = _mlp_and_norms(h32, hb, mix_fn, ln_mix_g, ln_mix_b, mlp_w1, mlp_w2_b, ln_ffn_g, ln_ffn_b,
                                 layer=layer, alpha=alpha)
    return h32.reshape(bsz, s, d_model)
```

```python
import functools
import math

import jax
import jax.numpy as jnp
from jax import lax
from jax.experimental import pallas as pl
from jax.experimental.pallas import tpu as pltpu

F32 = jnp.float32
BF16 = jnp.bfloat16

HEAD_DIM = 128
LN_EPS = 1e-5
LOG2E = math.log2(math.e)
LANES = 128
SUBLANES = 8
VMEM_CAP = 56 * 1024 * 1024
CUM_PIECES = 3
EXP2_CLAMP = 126.0

_NT = (((1,), (1,)), ((), ()))


def _vmem_limit(estimate_bytes):
    return int(min(VMEM_CAP, max(32 * 1024 * 1024, estimate_bytes * 5 // 4)))


def _log_sigmoid(x):
    return jnp.minimum(x, 0.0) - jnp.log(1.0 + jnp.exp(-jnp.abs(x)))


def _gelu_tanh(x):
    c = math.sqrt(2.0 / math.pi)
    return 0.5 * x * (1.0 + jnp.tanh(c * (x + 0.044715 * (x * x * x))))


def _epilogue(acc, e_ref, o_ref, epilogue, alpha):
    if epilogue == "cast":
        o_ref[...] = acc.astype(o_ref.dtype)
    elif epilogue == "colscale":
        o_ref[...] = (acc * e_ref[...]).astype(o_ref.dtype)
    elif epilogue == "relu2":
        r = jnp.maximum(acc, 0.0)
        o_ref[...] = (r * r).astype(o_ref.dtype)
    elif epilogue == "glu":
        g = e_ref[...].astype(F32)
        o_ref[...] = (g * (1.0 / (1.0 + jnp.exp(-acc)))).astype(o_ref.dtype)
    elif epilogue == "resid":
        o_ref[...] = (alpha * e_ref[...].astype(F32) + acc).astype(o_ref.dtype)
    else:
        raise ValueError(epilogue)


def _mm_w32_kernel(*refs, n_a, epilogue, alpha, has_extra, b_is_nk):
    a_refs = refs[:n_a]
    b_ref = refs[n_a]
    e_ref = refs[n_a + 1] if has_extra else None
    o_ref, wb_ref = refs[-2], refs[-1]

    @pl.when(pl.program_id(1) == 0)
    def _():
        w = b_ref[...].T if b_is_nk else b_ref[...]
        wb_ref[...] = w.astype(BF16)

    acc = None
    k0 = 0
    for a_ref in a_refs:
        kw = a_ref.shape[1]
        part = jnp.dot(a_ref[...], wb_ref[k0:k0 + kw, :], preferred_element_type=F32)
        acc = part if acc is None else acc + part
        k0 += kw
    _epilogue(acc, e_ref, o_ref, epilogue, alpha)


def _mm_ktiled_kernel(a_ref, b_ref, e_ref, o_ref, acc_ref, *, nk, epilogue, alpha):
    k = pl.program_id(2)

    @pl.when(k == 0)
    def _():
        acc_ref[...] = jnp.zeros_like(acc_ref)

    acc_ref[...] += jnp.dot(a_ref[...], b_ref[...], preferred_element_type=F32)

    @pl.when(k == nk - 1)
    def _():
        _epilogue(acc_ref[...], e_ref, o_ref, epilogue, alpha)


def _matmul_ktiled(a, b, layer, *, tm, tn, tk, epilogue, extra, alpha, out_dtype, name):
    m, kdim = a.shape
    _, k2, n = b.shape
    assert kdim == k2
    tm, tn, tk = min(tm, m), min(tn, n), min(tk, kdim)
    assert m % tm == 0 and n % tn == 0 and kdim % tk == 0
    nk = kdim // tk
    est = 2 * (tm * tk + tk * tn) * 2 + 2 * tm * tn * (jnp.dtype(out_dtype).itemsize + extra.dtype.itemsize)
    est += 2 * tm * tn * 4
    return pl.pallas_call(
        functools.partial(_mm_ktiled_kernel, nk=nk, epilogue=epilogue, alpha=alpha),
        grid=(m // tm, n // tn, nk),
        in_specs=[pl.BlockSpec((tm, tk), lambda i, j, k: (i, k)),
                  pl.BlockSpec((None, tk, tn), lambda i, j, k: (layer, k, j)),
                  pl.BlockSpec((tm, tn), lambda i, j, k: (i, j))],
        out_specs=pl.BlockSpec((tm, tn), lambda i, j, k: (i, j)),
        out_shape=jax.ShapeDtypeStruct((m, n), out_dtype),
        scratch_shapes=[pltpu.VMEM((tm, tn), F32)],
        compiler_params=pltpu.CompilerParams(
            dimension_semantics=("parallel", "parallel", "arbitrary"),
            vmem_limit_bytes=_vmem_limit(est)),
        name=name,
    )(a, b, extra)


def _matmul_w32(a_parts, b, layer, *, tm, tn, col_blocks=None, b_is_nk=False, epilogue="cast", extra=None,
                alpha=1.0, out_dtype=BF16, name):
    a_parts = tuple(a_parts)
    m = a_parts[0].shape[0]
    kdim, n_all = (b.shape[2], b.shape[1]) if b_is_nk else (b.shape[1], b.shape[2])
    assert sum(a.shape[1] for a in a_parts) == kdim
    tm, tn = min(tm, m), min(tn, n_all)
    nj = n_all // tn if col_blocks is None else col_blocks
    n = nj * tn
    assert m % tm == 0 and n <= n_all
    has_extra = extra is not None
    in_specs = [pl.BlockSpec((tm, a.shape[1]), lambda j, i: (i, 0)) for a in a_parts]
    if b_is_nk:
        in_specs.append(pl.BlockSpec((None, tn, kdim), lambda j, i: (layer, j, 0)))
    else:
        in_specs.append(pl.BlockSpec((None, kdim, tn), lambda j, i: (layer, 0, j)))
    args = list(a_parts) + [b]
    est = 2 * tm * kdim * 2 + 2 * kdim * tn * 4 + kdim * tn * 2
    est += 2 * tm * tn * jnp.dtype(out_dtype).itemsize + 2 * tm * tn * 4
    if has_extra:
        if epilogue == "colscale":
            in_specs.append(pl.BlockSpec((1, tn), lambda j, i: (0, j)))
        else:
            in_specs.append(pl.BlockSpec((tm, tn), lambda j, i: (i, j)))
            est += 2 * tm * tn * extra.dtype.itemsize
        args.append(extra)
    return pl.pallas_call(
        functools.partial(_mm_w32_kernel, n_a=len(a_parts), epilogue=epilogue, alpha=alpha,
                          has_extra=has_extra, b_is_nk=b_is_nk),
        grid=(nj, m // tm),
        in_specs=in_specs,
        out_specs=pl.BlockSpec((tm, tn), lambda j, i: (i, j)),
        out_shape=jax.ShapeDtypeStruct((m, n), out_dtype),
        scratch_shapes=[pltpu.VMEM((kdim, tn), BF16)],
        compiler_params=pltpu.CompilerParams(
            dimension_semantics=("parallel", "arbitrary"),
            vmem_limit_bytes=_vmem_limit(est)),
        name=name,
    )(*args)


def _query_colscale(n_cols, q_start, q_width):
    col = jnp.arange(n_cols)
    is_q = (col >= q_start) & (col < q_start + q_width)
    return jnp.where(is_q, HEAD_DIM ** -0.5 * LOG2E, 1.0).astype(F32).reshape(1, n_cols)


def _ln_kernel(x_ref, g_ref, b_ref, of_ref, ob_ref):
    x = x_ref[...]
    mu = jnp.mean(x, axis=-1, keepdims=True)
    xc = x - mu
    var = jnp.mean(xc * xc, axis=-1, keepdims=True)
    y = xc * lax.rsqrt(var + LN_EPS) * g_ref[...] + b_ref[...]
    of_ref[...] = y
    ob_ref[...] = y.astype(BF16)


def _layer_norm(x, g, b, *, tr, name):
    m, d = x.shape
    tr = min(tr, m)
    assert m % tr == 0
    row = pl.BlockSpec((tr, d), lambda i: (i, 0))
    vec = pl.BlockSpec((1, d), lambda i: (0, 0))
    return pl.pallas_call(
        _ln_kernel,
        grid=(m // tr,),
        in_specs=[row, vec, vec],
        out_specs=[row, row],
        out_shape=[jax.ShapeDtypeStruct((m, d), F32), jax.ShapeDtypeStruct((m, d), BF16)],
        compiler_params=pltpu.CompilerParams(
            dimension_semantics=("parallel",),
            vmem_limit_bytes=_vmem_limit(2 * tr * d * (4 + 4 + 2) + 4 * tr * d * 4)),
        name=name,
    )(x, g.reshape(1, d).astype(F32), b.reshape(1, d).astype(F32))


def _ssm_kernel(u_ref, bm_ref, cm_ref, ar_ref, ai_ref, d_ref, o_ref, sre, sim, st_ref,
                *, lc, pitch, nq, tiles_per_q, n_vregs):
    c = pl.program_id(1)

    @pl.when(c == 0)
    def _():
        st_ref[...] = jnp.zeros_like(st_ref)

    for q in range(nq):
        uq = u_ref[0, :, q * LANES:(q + 1) * LANES]
        res = jnp.dot(uq, bm_ref[q], preferred_element_type=F32)
        for tl in range(tiles_per_q):
            base = (q * tiles_per_q + tl) * pitch
            sre[base:base + lc, :] = res[:, (2 * tl) * LANES:(2 * tl + 1) * LANES]
            sim[base:base + lc, :] = res[:, (2 * tl + 1) * LANES:(2 * tl + 2) * LANES]

    ar = [ar_ref[j * SUBLANES:(j + 1) * SUBLANES, :] for j in range(n_vregs)]
    ai = [ai_ref[j * SUBLANES:(j + 1) * SUBLANES, :] for j in range(n_vregs)]
    xr0 = tuple(st_ref[0, j * SUBLANES:(j + 1) * SUBLANES, :] for j in range(n_vregs))
    xi0 = tuple(st_ref[1, j * SUBLANES:(j + 1) * SUBLANES, :] for j in range(n_vregs))

    def step(t, carry):
        xr, xi = carry
        nxr, nxi = [], []
        for j in range(n_vregs):
            rows = pl.ds(j * SUBLANES * pitch + t, SUBLANES, stride=pitch)
            nr = ar[j] * xr[j] - ai[j] * xi[j] + sre[rows, :]
            ni = ar[j] * xi[j] + ai[j] * xr[j] + sim[rows, :]
            sre[rows, :] = nr
            sim[rows, :] = ni
            nxr.append(nr)
            nxi.append(ni)
        return tuple(nxr), tuple(nxi)

    xr, xi = lax.fori_loop(0, lc, step, (xr0, xi0), unroll=2)
    for j in range(n_vregs):
        st_ref[0, j * SUBLANES:(j + 1) * SUBLANES, :] = xr[j]
        st_ref[1, j * SUBLANES:(j + 1) * SUBLANES, :] = xi[j]

    for q in range(nq):
        pieces = []
        for tl in range(tiles_per_q):
            base = (q * tiles_per_q + tl) * pitch
            pieces.append(sre[base:base + lc, :].astype(BF16))
            pieces.append(sim[base:base + lc, :].astype(BF16))
        xcat = jnp.concatenate(pieces, axis=1)
        y = jnp.dot(xcat, cm_ref[q], preferred_element_type=F32)
        uq = u_ref[0, :, q * LANES:(q + 1) * LANES].astype(F32)
        y = y + d_ref[:, q * LANES:(q + 1) * LANES] * uq
        o_ref[0, :, q * LANES:(q + 1) * LANES] = _gelu_tanh(y).astype(o_ref.dtype)


def _ssm_discretise(a_re, a_im, log_dt, b_re, b_im, c_re, c_im):
    g, n = a_re.shape
    p = b_re.shape[-1]
    lr = a_re.astype(F32)
    li = a_im.astype(F32)
    dt = jnp.exp(log_dt.astype(F32))[:, None]
    mag = jnp.exp(lr * dt)
    abar_r = mag * jnp.cos(li * dt)
    abar_i = mag * jnp.sin(li * dt)
    den = lr * lr + li * li
    nr = abar_r - 1.0
    ni = abar_i
    zr = (nr * lr + ni * li) / den
    zi = (ni * lr - nr * li) / den
    br = b_re.astype(F32)
    bi = b_im.astype(F32)
    bbar_r = zr[..., None] * br - zi[..., None] * bi
    bbar_i = zr[..., None] * bi + zi[..., None] * br

    gq = LANES // p
    gt = LANES // n
    nq = g // gq
    tq = gq // gt
    eye = jnp.eye(gq, dtype=F32)
    bb = jnp.stack([bbar_r, bbar_i], 0).reshape(2, nq, gq, n, p)
    bf = jnp.einsum("aqhnp,gh->qgpahn", bb, eye)
    bf = bf.reshape(nq, gq, p, 2, tq, gt, n).transpose(0, 1, 2, 4, 3, 5, 6)
    bmat = bf.reshape(nq, gq * p, tq * 2 * gt * n).astype(BF16)
    cc = jnp.stack([c_re.astype(F32), -c_im.astype(F32)], 0).reshape(2, nq, gq, p, n)
    cf = jnp.einsum("aqhpn,gh->qahngp", cc, eye)
    cf = cf.reshape(nq, 2, tq, gt, n, gq, p).transpose(0, 2, 1, 3, 4, 5, 6)
    cmat = cf.reshape(nq, tq * 2 * gt * n, gq * p).astype(BF16)
    n_tiles = g * n // LANES
    return bmat, cmat, abar_r.reshape(n_tiles, LANES), abar_i.reshape(n_tiles, LANES)


def _ssm_scan(proj, bmat, cmat, ar, ai, d, *, w_ssm, lc):
    bsz, s, _ = proj.shape
    lc = min(lc, s)
    assert s % lc == 0 and lc % SUBLANES == 0
    nq, _, qcols = bmat.shape
    tiles_per_q = qcols // (2 * LANES)
    n_tiles = ar.shape[0]
    assert n_tiles == nq * tiles_per_q and n_tiles % SUBLANES == 0 and w_ssm == nq * LANES
    n_vregs = n_tiles // SUBLANES
    pitch = lc + 4
    est = (2 * 2 * lc * w_ssm * 2 + 2 * (bmat.size + cmat.size) * 2
           + 2 * n_tiles * pitch * LANES * 4 + 4 * lc * qcols * 4)
    kern = functools.partial(_ssm_kernel, lc=lc, pitch=pitch, nq=nq, tiles_per_q=tiles_per_q,
                             n_vregs=n_vregs)
    const2 = lambda b, c: (0, 0)
    const3 = lambda b, c: (0, 0, 0)
    return pl.pallas_call(
        kern,
        grid=(bsz, s // lc),
        in_specs=[pl.BlockSpec((1, lc, w_ssm), lambda b, c: (b, c, 0)),
                  pl.BlockSpec(bmat.shape, const3),
                  pl.BlockSpec(cmat.shape, const3),
                  pl.BlockSpec(ar.shape, const2),
                  pl.BlockSpec(ai.shape, const2),
                  pl.BlockSpec((1, w_ssm), const2)],
        out_specs=pl.BlockSpec((1, lc, w_ssm), lambda b, c: (b, c, 0)),
        out_shape=jax.ShapeDtypeStruct((bsz, s, w_ssm), BF16),
        scratch_shapes=[pltpu.VMEM((n_tiles * pitch, LANES), F32),
                        pltpu.VMEM((n_tiles * pitch, LANES), F32),
                        pltpu.VMEM((2, n_tiles, LANES), F32)],
        compiler_params=pltpu.CompilerParams(
            dimension_semantics=("parallel", "arbitrary"),
            vmem_limit_bytes=_vmem_limit(est)),
        name="s5_scan",
    )(proj, bmat, cmat, ar, ai, d.reshape(1, w_ssm).astype(F32))


def _attn_tiles(s, tq, tk):
    tq, tk = min(tq, s), min(tk, s)
    assert s % tq == 0 and tq % (2 * tk) == 0
    return tq, tk


def _key_query_iota(tk, tq):
    key = lax.broadcasted_iota(jnp.int32, (tk, tq), 0)
    query = lax.broadcasted_iota(jnp.int32, (tk, tq), 1)
    return key, query


def _transpose_values(v_ref, vt_ref, hs, chunk=512):
    s = v_ref.shape[1]
    chunk = min(chunk, s)

    @pl.when(pl.program_id(2) == 0)
    def _():
        for h in range(hs):
            for c in range(s // chunk):
                rows = slice(c * chunk, (c + 1) * chunk)
                vt_ref[h, :, rows] = v_ref[0, rows, h * HEAD_DIM:(h + 1) * HEAD_DIM].T


def _sb_kernel(q_ref, k_ref, v_ref, o_ref, z_ref, hi_ref, lo_ref, acc_ref, vt_ref, carry_ref, *, tq, tk, hs):
    qi = pl.program_id(2)
    r = tq // tk
    heads = range(hs)
    kk = lax.broadcasted_iota(jnp.int32, (tk, tk), 0)
    kk2 = lax.broadcasted_iota(jnp.int32, (tk, tk), 1)
    at_or_after = (kk2 >= kk).astype(BF16)

    def head_cols(h):
        return slice(h * HEAD_DIM, (h + 1) * HEAD_DIM)

    q = [q_ref[0, :, head_cols(h)] for h in heads]
    _transpose_values(v_ref, vt_ref, hs)

    def keep_logs(h, slot, kb, q0=0, mask=None):
        k = k_ref[0, pl.ds(pl.multiple_of(kb * tk, tk), tk), head_cols(h)]
        z = lax.dot_general(k, q[h][q0:, :], _NT, preferred_element_type=F32)
        nk = jnp.maximum(z, jnp.log2(1.0 + jnp.exp2(jnp.minimum(z, EXP2_CLAMP))))
        if mask is not None:
            nk = jnp.where(mask, nk, 0.0)
        hi = nk.astype(BF16)
        z_ref[h, slot, :, q0:] = z
        hi_ref[h, slot, :, q0:] = hi
        lo_ref[h, slot, :, q0:] = (nk - hi.astype(F32)).astype(BF16)

    def accumulate(h, slot, kb, carry, q0=0, mask=None):
        vt = vt_ref[h, :, pl.ds(pl.multiple_of(kb * tk, tk), tk)]
        incl = (jnp.dot(at_or_after, hi_ref[h, slot, :, q0:], preferred_element_type=F32)
                + jnp.dot(at_or_after, lo_ref[h, slot, :, q0:], preferred_element_type=F32))
        w = jnp.exp2(z_ref[h, slot, :, q0:] - incl - carry)
        if mask is not None:
            w = jnp.where(mask, w, 0.0)
        acc_ref[h, :, q0:] += jnp.dot(vt, w.astype(BF16), preferred_element_type=F32)
        return carry + incl[0:1, :]

    carry_ref[...] = jnp.zeros_like(carry_ref)
    acc_ref[...] = jnp.zeros_like(acc_ref)
    for d in reversed(range(r)):
        q0 = d * tk
        key, query = _key_query_iota(tk, tq - q0)
        mask = key < query
        for h in heads:
            keep_logs(h, d % 2, qi * r + d, q0, mask)
        for h in heads:
            carry_ref[h, :, q0:] = accumulate(h, d % 2, qi * r + d, carry_ref[h, :, q0:], q0, mask)
    carry = [carry_ref[h] for h in heads]

    n_full = qi * r
    for h in heads:
        keep_logs(h, 0, jnp.maximum(n_full - 1, 0))

    def body(it, carry):
        kb = n_full - 1 - 2 * it
        for h in heads:
            keep_logs(h, 1, kb - 1)
        carry = [accumulate(h, 0, kb, carry[h]) for h in heads]
        for h in heads:
            keep_logs(h, 0, jnp.maximum(kb - 2, 0))
        return [accumulate(h, 1, kb - 1, carry[h]) for h in heads]

    lax.fori_loop(0, n_full // 2, body, carry)
    for h in heads:
        o_ref[0, :, head_cols(h)] = acc_ref[h].T.astype(o_ref.dtype)


def _sb_attention(proj, *, heads, col0, tq, tk):
    bsz, s, _ = proj.shape
    tq, tk = _attn_tiles(s, tq, tk)
    hs = min(SB_HEADS_PER_STEP, heads)
    wcols = hs * HEAD_DIM
    assert heads % hs == 0 and col0 % wcols == 0 and (heads * HEAD_DIM) % wcols == 0
    c0, hg = col0 // wcols, heads // hs
    est = 2 * (2 * s * wcols * 2) + 4 * tq * wcols * 2 + hs * 16 * tq * tk * 4
    return pl.pallas_call(
        functools.partial(_sb_kernel, tq=tq, tk=tk, hs=hs),
        grid=(bsz, hg, s // tq),
        in_specs=[pl.BlockSpec((1, tq, wcols), lambda b, h, i: (b, i, c0 + h)),
                  pl.BlockSpec((1, s, wcols), lambda b, h, i: (b, 0, c0 + hg + h)),
                  pl.BlockSpec((1, s, wcols), lambda b, h, i: (b, 0, c0 + 2 * hg + h))],
        out_specs=pl.BlockSpec((1, tq, wcols), lambda b, h, i: (b, i, h)),
        out_shape=jax.ShapeDtypeStruct((bsz, s, heads * HEAD_DIM), BF16),
        scratch_shapes=[pltpu.VMEM((hs, 2, tk, tq), F32), pltpu.VMEM((hs, 2, tk, tq), BF16),
                        pltpu.VMEM((hs, 2, tk, tq), BF16), pltpu.VMEM((hs, HEAD_DIM, tq), F32),
                        pltpu.VMEM((hs, HEAD_DIM, s), BF16), pltpu.VMEM((hs, 1, tq), F32)],
        compiler_params=pltpu.CompilerParams(
            dimension_semantics=("parallel", "parallel", "arbitrary"),
            vmem_limit_bytes=_vmem_limit(est)),
        name="stickbreak_attn",
    )(proj, proj, proj)


def _cum_kernel(f_ref, b_ref, o_ref, kc_ref, carry_ref, *, chunk, heads):
    ts, w = f_ref.shape[1], f_ref.shape[2]

    @pl.when(pl.program_id(1) == 0)
    def _():
        carry_ref[...] = jnp.zeros_like(carry_ref)

    row = lax.broadcasted_iota(jnp.int32, (chunk, chunk), 0)
    col = lax.broadcasted_iota(jnp.int32, (chunk, chunk), 1)
    tri = (col <= row).astype(F32)
    src = lax.broadcasted_iota(jnp.int32, (CUM_PIECES * w, heads * HEAD_DIM), 0)
    dst = lax.broadcasted_iota(jnp.int32, (CUM_PIECES * w, heads * HEAD_DIM), 1)
    place = ((src % w) * HEAD_DIM + src // w == dst).astype(BF16)
    carry = carry_ref[...]
    for c in range(ts // chunk):
        rows = slice(c * chunk, (c + 1) * chunk)
        ls = _log_sigmoid(f_ref[0, rows, :] + b_ref[...])
        cs = jnp.dot(tri, ls, preferred_element_type=F32, precision=lax.Precision.HIGHEST) + carry
        carry = cs[chunk - 1:chunk, :]
        o_ref[0, rows, :] = cs * LOG2E
    carry_ref[...] = carry
    rest = -o_ref[0]
    pieces = []
    for _ in range(CUM_PIECES):
        piece = rest.astype(BF16)
        pieces.append(piece)
        rest = rest - piece.astype(F32)
    kc_ref[0] = jnp.dot(jnp.concatenate(pieces, axis=1), place, preferred_element_type=F32).astype(BF16)


def _forget_cumsum(f_logit, b_f, *, heads, chunk=256, ts=512):
    bsz, s, w = f_logit.shape
    ts = min(ts, s)
    chunk = min(chunk, ts)
    assert s % ts == 0 and ts % chunk == 0 and heads <= w
    return pl.pallas_call(
        functools.partial(_cum_kernel, chunk=chunk, heads=heads),
        grid=(bsz, s // ts),
        in_specs=[pl.BlockSpec((1, ts, w), lambda b, t: (b, t, 0)), pl.BlockSpec((1, w), lambda b, t: (0, 0))],
        out_specs=[pl.BlockSpec((1, ts, w), lambda b, t: (b, t, 0)),
                   pl.BlockSpec((1, ts, heads * HEAD_DIM), lambda b, t: (b, t, 0))],
        out_shape=[jax.ShapeDtypeStruct((bsz, s, w), F32),
                   jax.ShapeDtypeStruct((bsz, s, heads * HEAD_DIM), BF16)],
        scratch_shapes=[pltpu.VMEM((1, w), F32)],
        compiler_params=pltpu.CompilerParams(dimension_semantics=("parallel", "arbitrary")),
        name="forget_cumsum",
    )(f_logit, b_f)


def _fox_kernel(q_ref, k_ref, kc_ref, v_ref, ci_ref, o_ref, z_ref, p_ref, acc_ref, vt_ref, st_ref,
                *, tq, tk, hs):
    qi = pl.program_id(2)
    r = tq // tk
    heads = range(hs)
    lane = lax.broadcasted_iota(jnp.int32, (tq, HEAD_DIM), 1)
    ones = (lane < CUM_PIECES).astype(BF16)

    def head_cols(h):
        return slice(h * HEAD_DIM, (h + 1) * HEAD_DIM)

    q = [jnp.concatenate([q_ref[0, :, head_cols(h)], ones], axis=1) for h in heads]
    _transpose_values(v_ref, vt_ref, hs)

    def logits(h, kb, q0=0):
        rows = pl.ds(pl.multiple_of(kb * tk, tk), tk)
        k = jnp.concatenate([k_ref[0, rows, head_cols(h)], kc_ref[0, rows, head_cols(h)]], axis=1)
        return lax.dot_general(k, q[h][q0:, :], _NT, preferred_element_type=F32)

    def softmax(h, z, st, q0=0):
        _, m, l = st
        cq = ci_ref[0, h, pl.ds(qi, 1), q0:]
        m_new = jnp.maximum(m, jnp.max(z, axis=0, keepdims=True) + cq)
        p = jnp.exp2(z - (m_new - cq))
        a = jnp.exp2(m - m_new)
        return p.astype(BF16), (a, m_new, a * l + jnp.sum(p, axis=0, keepdims=True))

    def rescale_add(h, st, p, kb):
        vt = vt_ref[h, :, pl.ds(pl.multiple_of(kb * tk, tk), tk)]
        return st[0] * acc_ref[h] + jnp.dot(vt, p, preferred_element_type=F32)

    st_ref[:, 0] = jnp.ones_like(st_ref[:, 0])
    st_ref[:, 1] = jnp.full_like(st_ref[:, 1], -jnp.inf)
    st_ref[:, 2] = jnp.zeros_like(st_ref[:, 2])
    acc_ref[...] = jnp.zeros_like(acc_ref)
    for d in range(r):
        q0 = d * tk
        key, query = _key_query_iota(tk, tq - q0)
        mask = key <= query
        z = [jnp.where(mask, logits(h, qi * r + d, q0), -jnp.inf) for h in heads]
        if d > 0:
            for h in heads:
                acc_ref[h] = rescale_add(h, (st_ref[h, 0],), p_ref[h, (d - 1) % 2], qi * r + d - 1)
        for h in heads:
            if q0:
                p_ref[h, d % 2, :, :q0] = jnp.zeros((tk, q0), BF16)
                st_ref[h, 0, :, :q0] = jnp.ones((1, q0), F32)
            p, (a, m, l) = softmax(h, z[h], (None, st_ref[h, 1, :, q0:], st_ref[h, 2, :, q0:]), q0)
            p_ref[h, d % 2, :, q0:] = p
            st_ref[h, 0, :, q0:] = a
            st_ref[h, 1, :, q0:] = m
            st_ref[h, 2, :, q0:] = l
    st = [(st_ref[h, 0], st_ref[h, 1], st_ref[h, 2]) for h in heads]

    n_full = qi * r
    for h in heads:
        z_ref[h, 0] = logits(h, jnp.maximum(n_full - 1, 0))

    def step(slot, kb, kb_p, st):
        st = list(st)
        for h in heads:
            z_ref[h, 1 - slot] = logits(h, jnp.maximum(kb - 1, 0))
        for h in heads:
            acc_ref[h] = rescale_add(h, st[h], p_ref[h, 1 - slot], kb_p)
        for h in heads:
            p_ref[h, slot], st[h] = softmax(h, z_ref[h, slot], st[h])
        return st

    def body(it, st):
        kb = n_full - 1 - 2 * it
        st = step(0, kb, jnp.where(it == 0, qi * r + r - 1, kb + 1), st)
        return step(1, kb - 1, kb, st)

    st = lax.fori_loop(0, n_full // 2, body, st)
    for h in heads:
        acc = rescale_add(h, st[h], p_ref[h, 1], jnp.where(qi == 0, r - 1, 0))
        o_ref[0, :, head_cols(h)] = (acc / st[h][2]).T.astype(o_ref.dtype)


def _fox_attention(qkv, kc, cum_t, *, heads, tq, tk):
    bsz, s, _ = qkv.shape
    tq, tk = _attn_tiles(s, tq, tk)
    nq = s // tq
    ci = cum_t.reshape(bsz, heads, nq, tq)
    hs = min(FOX_HEADS_PER_STEP, heads)
    wcols = hs * HEAD_DIM
    assert heads % hs == 0
    hg = heads // hs
    est = 2 * (3 * s * wcols * 2) + 4 * tq * wcols * 2 + hs * 16 * tq * tk * 4
    return pl.pallas_call(
        functools.partial(_fox_kernel, tq=tq, tk=tk, hs=hs),
        grid=(bsz, hg, nq),
        in_specs=[pl.BlockSpec((1, tq, wcols), lambda b, h, i: (b, i, h)),
                  pl.BlockSpec((1, s, wcols), lambda b, h, i: (b, 0, hg + h)),
                  pl.BlockSpec((1, s, wcols), lambda b, h, i: (b, 0, h)),
                  pl.BlockSpec((1, s, wcols), lambda b, h, i: (b, 0, 2 * hg + h)),
                  pl.BlockSpec((1, hs, nq, tq), lambda b, h, i: (b, h, 0, 0))],
        out_specs=pl.BlockSpec((1, tq, wcols), lambda b, h, i: (b, i, h)),
        out_shape=jax.ShapeDtypeStruct((bsz, s, heads * HEAD_DIM), BF16),
        scratch_shapes=[pltpu.VMEM((hs, 2, tk, tq), F32), pltpu.VMEM((hs, 2, tk, tq), BF16),
                        pltpu.VMEM((hs, HEAD_DIM, tq), F32), pltpu.VMEM((hs, HEAD_DIM, s), BF16),
                        pltpu.VMEM((hs, 3, 1, tq), F32)],
        compiler_params=pltpu.CompilerParams(
            dimension_semantics=("parallel", "parallel", "arbitrary"),
            vmem_limit_bytes=_vmem_limit(est)),
        name="fox_attn",
    )(qkv, qkv, kc, qkv, ci)


MM_TM, MM_TN = 1024, 512
KT_TM, KT_TN, KT_TK = 1024, 1024, 2048
LN_ROWS = 256
SSM_CHUNK = 256
SB_TQ, SB_TK, SB_HEADS_PER_STEP = 1024, 256, 4
FOX_TQ, FOX_TK, FOX_HEADS_PER_STEP = 512, 256, 4


def _mlp_and_norms(h32, hb, mix_fn, ln_mix_g, ln_mix_b, w1, w2, ln_ffn_g, ln_ffn_b, *, layer, alpha):
    z = mix_fn(h32, hb)
    h32, hb = _layer_norm(z, ln_mix_g[layer], ln_mix_b[layer], tr=LN_ROWS, name=f"ln_mix_{layer}")
    hid = _matmul_w32([hb], w1, layer, tm=MM_TM, tn=MM_TN, epilogue="relu2", name=f"mlp_up_{layer}")
    z = _matmul_ktiled(hid, w2, layer, tm=KT_TM, tn=KT_TN, tk=KT_TK, epilogue="resid", extra=h32,
                       alpha=alpha, out_dtype=F32, name=f"mlp_down_{layer}")
    return _layer_norm(z, ln_ffn_g[layer], ln_ffn_b[layer], tr=LN_ROWS, name=f"ln_ffn_{layer}")


def kernel(x, even_w_in, ssm_a_re, ssm_a_im, ssm_log_dt, ssm_b_re, ssm_b_im, ssm_c_re, ssm_c_im, ssm_d,
           ssm_w_glu, even_w_out, fox_w_in, fox_b_f, fox_w_out, ln_mix_g, ln_mix_b, mlp_w1, mlp_w2,
           ln_ffn_g, ln_ffn_b):
    bsz, s, d_model = x.shape
    depth = ln_mix_g.shape[0]
    alpha = (2.0 * depth) ** 0.25
    m = bsz * s
    w_ssm = ssm_d.shape[-1]

    h32 = x.reshape(m, d_model)
    hb = h32.astype(BF16)
    mlp_w2_b = mlp_w2.astype(BF16)
    for layer in range(depth):
        i = layer // 2
        if layer % 2 == 0:
            def mix_fn(h32, hb, i=i):
                n_in = even_w_in.shape[-1]
                sb_heads = (n_in - w_ssm) // (3 * HEAD_DIM)
                proj = _matmul_w32([hb], even_w_in, i, tm=MM_TM, tn=MM_TN, epilogue="colscale",
                                   extra=_query_colscale(n_in, w_ssm, sb_heads * HEAD_DIM), name="even_in")
                proj3 = proj.reshape(bsz, s, -1)
                bmat, cmat, ar, ai = _ssm_discretise(ssm_a_re[i], ssm_a_im[i], ssm_log_dt[i], ssm_b_re[i],
                                                     ssm_b_im[i], ssm_c_re[i], ssm_c_im[i])
                g = _ssm_scan(proj3, bmat, cmat, ar, ai, ssm_d[i], w_ssm=w_ssm, lc=SSM_CHUNK).reshape(m, w_ssm)
                y_ssm = _matmul_w32([g], ssm_w_glu, i, tm=MM_TM, tn=MM_TN, epilogue="glu", extra=g,
                                    name="ssm_glu")
                y_sb = _sb_attention(proj3, heads=sb_heads, col0=w_ssm, tq=SB_TQ, tk=SB_TK).reshape(m, -1)
                return _matmul_w32([y_ssm, y_sb], even_w_out, i, tm=MM_TM, tn=MM_TN, epilogue="resid",
                                   extra=h32, alpha=alpha, out_dtype=F32, name="even_out")
        else:
            def mix_fn(h32, hb, i=i):
                heads = fox_b_f.shape[-1]
                width = heads * HEAD_DIM
                w_nk = jnp.swapaxes(fox_w_in, 1, 2)
                tn = min(MM_TN, 3 * width)
                qkv = _matmul_w32([hb], w_nk, i, tm=MM_TM, tn=tn, col_blocks=3 * width // tn, b_is_nk=True,
                                  epilogue="colscale", extra=_query_colscale(3 * width, 0, width), name="fox_in")
                qkv = qkv.reshape(bsz, s, 3 * width)
                w_f = jnp.pad(w_nk[i:i + 1, 3 * width:, :], ((0, 0), (0, LANES - heads), (0, 0)))
                b_f = jnp.pad(fox_b_f[i].astype(F32), (0, LANES - heads)).reshape(1, LANES)
                f_logit = _matmul_w32([hb], w_f, 0, tm=MM_TM, tn=LANES, b_is_nk=True, out_dtype=F32,
                                      name="fox_forget")
                cum, kc = _forget_cumsum(f_logit.reshape(bsz, s, LANES), b_f, heads=heads)
                cum_t = cum[:, :, :heads].transpose(0, 2, 1)
                y = _fox_attention(qkv, kc, cum_t, heads=heads, tq=FOX_TQ, tk=FOX_TK)
                return _matmul_w32([y.reshape(m, width)], fox_w_out, i, tm=MM_TM, tn=MM_TN, epilogue="resid",
                                   extra=h32, alpha=alpha, out_dtype=F32, name="fox_out")
        h32, hb = _mlp_and_norms(h32, hb, mix_fn, ln_mix_g, ln_mix_b, mlp_w1, mlp_w2_b, ln_ffn_g, ln_ffn_b,
                                 layer=layer, alpha=alpha)
    return h32.reshape(bsz, s, d_model)
```
